```python
import jax, jax.numpy as jnp
from jax import lax
import numpy as np

D_MODEL = 1024
BATCH = 32
SEQ = 2048
DEPTH = 1

GRID_W = 64
CTX_LEN = 256

N_HEADS = 8
QK_NOPE_DIM = 128
ROPE_DIM = 64
V_HEAD_DIM = 128
QK_HEAD_DIM = QK_NOPE_DIM + ROPE_DIM
Q_LORA_RANK = 256
KV_LORA_RANK = 128
ROPE_BASE = 10000.0
ATTN_SCALE = QK_HEAD_DIM ** -0.5
Q_BLOCK = 128

POOL_WINDOWS = (2, 4, 8, 16)
POOL_GROUPS = 4
POOL_WIDTH = D_MODEL
POOL_GROUP_DIM = POOL_WIDTH // POOL_GROUPS

N_BRANCHES = 2
KV_IN = KV_LORA_RANK + ROPE_DIM
Q_OFF = KV_IN
POOL_OFF = Q_OFF + Q_LORA_RANK
GATE_OFF = POOL_OFF + POOL_WIDTH
IN_WIDTH = GATE_OFF + N_BRANCHES * D_MODEL

PEER_HEADS = 8
N_KEYS = 128
N_EXPERTS = N_KEYS * N_KEYS
PEER_QDIM = 256
PEER_HALF = PEER_QDIM // 2
PEER_TOPK = 16
TOKEN_CHUNK = 128

EPS = 1e-6

kernel_name = "hybrid_mla_pool_peer_dit_block"


def rmsnorm(x, g):
    xf = x.astype(jnp.float32)
    var = jnp.mean(jnp.square(xf), axis=-1, keepdims=True)
    return (xf * lax.rsqrt(var + EPS)).astype(x.dtype) * g


def modulate(h, shift, scale):
    return h * (1 + scale) + shift


def axial_rope_tables(rows, dtype):
    row = jnp.broadcast_to(jnp.arange(rows, dtype=jnp.float32)[:, None], (rows, GRID_W)).reshape(-1)
    col = jnp.broadcast_to(jnp.arange(GRID_W, dtype=jnp.float32)[None, :], (rows, GRID_W)).reshape(-1)
    n_freq = ROPE_DIM // 4
    inv_freq = ROPE_BASE ** (-jnp.arange(n_freq, dtype=jnp.float32) / n_freq)
    ang = jnp.stack([row[:, None] * inv_freq, col[:, None] * inv_freq], axis=1)
    return jnp.cos(ang).astype(dtype), jnp.sin(ang).astype(dtype)


def apply_axial_rope(x, cos, sin):
    xs = x.reshape(x.shape[:-1] + (2, 2, ROPE_DIM // 4))
    x1, x2 = xs[..., 0, :], xs[..., 1, :]
    out = jnp.stack([x1 * cos - x2 * sin, x1 * sin + x2 * cos], axis=-2)
    return out.reshape(x.shape)


def mla_kv(kv_in, g_kv_lora, w_ukv):
    ckv, k_rope = kv_in[..., :KV_LORA_RANK], kv_in[..., KV_LORA_RANK:]
    kv = rmsnorm(ckv, g_kv_lora) @ w_ukv
    kv = kv.reshape(kv.shape[:-1] + (N_HEADS, QK_NOPE_DIM + V_HEAD_DIM))
    return kv[..., :QK_NOPE_DIM], kv[..., QK_NOPE_DIM:], k_rope


def mla_q(cq, g_q_lora, w_uq):
    q = (rmsnorm(cq, g_q_lora) @ w_uq).reshape(cq.shape[:-1] + (N_HEADS, QK_HEAD_DIM))
    return q[..., :QK_NOPE_DIM], q[..., QK_NOPE_DIM:]


def mla_attend(q_nope, q_rope, k_nope, k_rope, v):
    s = (jnp.einsum('bqhd,bkhd->bhqk', q_nope, k_nope)
         + jnp.einsum('bqhr,bkr->bhqk', q_rope, k_rope))
    p = jax.nn.softmax(s.astype(jnp.float32) * ATTN_SCALE, axis=-1).astype(v.dtype)
    return jnp.einsum('bhqk,bkhd->bqhd', p, v)


def latent_attention(q_nope, q_rope, k_nope, k_rope, v):
    b, s = q_nope.shape[:2]
    nb = s // Q_BLOCK
    qn = q_nope.reshape(b, nb, Q_BLOCK, N_HEADS, QK_NOPE_DIM).transpose(1, 0, 2, 3, 4)
    qr = q_rope.reshape(b, nb, Q_BLOCK, N_HEADS, ROPE_DIM).transpose(1, 0, 2, 3, 4)
    out = lax.map(lambda blk: mla_attend(blk[0], blk[1], k_nope, k_rope, v), (qn, qr))
    return out.transpose(1, 0, 2, 3, 4).reshape(b, s, N_HEADS * V_HEAD_DIM)


def multiscale_pool(xp, w_pool, pool_scale):
    b, L, _ = xp.shape
    xg = xp.reshape(b, L, POOL_GROUPS, POOL_GROUP_DIM)
    cs = jnp.cumsum(xg.astype(jnp.float32), axis=1)
    cs = jnp.concatenate([jnp.zeros_like(cs[:, :1]), cs], axis=1)
    t = jnp.arange(L, dtype=jnp.int32)[:, None]
    half = (jnp.array(POOL_WINDOWS, dtype=jnp.int32) // 2)[None, :]
    lo = jnp.clip(t - half, 0, L)
    hi = jnp.clip(t + half, 0, L)
    gidx = jnp.arange(POOL_GROUPS, dtype=jnp.int32)[None, :]
    win_sum = cs[:, hi, gidx] - cs[:, lo, gidx]
    count = (hi - lo).astype(jnp.float32)[None, :, :, None]
    pooled = (win_sum / count - xg.astype(jnp.float32)).astype(xp.dtype)
    y = jnp.einsum('blgc,gcd->blgd', pooled, w_pool)
    return y.reshape(b, L, POOL_WIDTH) * pool_scale


def merge_branches(proj, attn_out, w_pool, pool_scale, w_out):
    y_pool = multiscale_pool(proj[..., POOL_OFF:GATE_OFF], w_pool, pool_scale)
    gates = jax.nn.sigmoid(proj[..., GATE_OFF:].astype(jnp.float32)).astype(proj.dtype)
    merged = gates[..., :D_MODEL] * attn_out + gates[..., D_MODEL:] * y_pool
    return merged @ w_out


def peer_ffn(h, peer_wq, peer_keys, peer_u, peer_v):
    shape = h.shape
    tokens = h.reshape(-1, TOKEN_CHUNK, D_MODEL)

    def chunk(xc):
        n_tok = xc.shape[0]
        q = (xc @ peer_wq).reshape(n_tok, PEER_HEADS, 2, PEER_HALF)
        s = jnp.einsum('thpc,pnc->thpn', q, peer_keys).astype(jnp.float32)
        s_top, i_top = lax.top_k(s, PEER_TOPK)
        cand = s_top[..., 0, :, None] + s_top[..., 1, None, :]
        cand_idx = i_top[..., 0, :, None] * N_KEYS + i_top[..., 1, None, :]
        best, pos = lax.top_k(cand.reshape(n_tok, PEER_HEADS, PEER_TOPK * PEER_TOPK), PEER_TOPK)
        experts = jnp.take_along_axis(
            cand_idx.reshape(n_tok, PEER_HEADS, PEER_TOPK * PEER_TOPK), pos, axis=-1)
        g = jax.nn.softmax(best, axis=-1).astype(xc.dtype)
        u = peer_u[experts]
        v = peer_v[experts]
        a = jax.nn.gelu(jnp.einsum('thkd,td->thk', u, xc), approximate=False) * g
        return jnp.einsum('thk,thkd->td', a, v)

    return lax.map(chunk, tokens).reshape(shape)


def setup_inputs(seed: int = 0) -> dict:
    key = jax.random.key(seed)
    ks = jax.random.split(key, 22)
    f32 = jnp.float32

    def nrm(k, shape, scale):
        return jax.random.normal(k, shape, f32) * scale

    def gain(k, shape):
        return 1.0 + 0.1 * jax.random.normal(k, shape, f32)

    return {
        "x": nrm(ks[0], (BATCH, SEQ, D_MODEL), 1.0),
        "c": nrm(ks[1], (BATCH, D_MODEL), 1.0),
        "ctx": nrm(ks[2], (BATCH, CTX_LEN, D_MODEL), 1.0),
        "c_ctx": nrm(ks[3], (D_MODEL,), 1.0),
        "w_ada": nrm(ks[4], (DEPTH, D_MODEL, 6 * D_MODEL), 0.5 * D_MODEL ** -0.5),
        "b_ada": nrm(ks[5], (DEPTH, 6 * D_MODEL), 0.01),
        "g_mix_pre": gain(ks[6], (DEPTH, D_MODEL)),
        "g_mix_post": gain(ks[7], (DEPTH, D_MODEL)),
        "g_ffn_pre": gain(ks[8], (DEPTH, D_MODEL)),
        "g_ffn_post": gain(ks[9], (DEPTH, D_MODEL)),
        "w_in": nrm(ks[10], (DEPTH, D_MODEL, IN_WIDTH), D_MODEL ** -0.5),
        "g_q_lora": gain(ks[11], (DEPTH, Q_LORA_RANK)),
        "w_uq": nrm(ks[12], (DEPTH, Q_LORA_RANK, N_HEADS * QK_HEAD_DIM), Q_LORA_RANK ** -0.5),
        "g_kv_lora": gain(ks[13], (DEPTH, KV_LORA_RANK)),
        "w_ukv": nrm(ks[14], (DEPTH, KV_LORA_RANK, N_HEADS * (QK_NOPE_DIM + V_HEAD_DIM)), KV_LORA_RANK ** -0.5),
        "w_pool": nrm(ks[15], (DEPTH, POOL_GROUPS, POOL_GROUP_DIM, POOL_GROUP_DIM), POOL_GROUP_DIM ** -0.5),
        "pool_scale": gain(ks[16], (DEPTH, POOL_WIDTH)),
        "w_out": nrm(ks[17], (DEPTH, D_MODEL, D_MODEL), D_MODEL ** -0.5),
        "peer_wq": nrm(ks[18], (DEPTH, D_MODEL, PEER_HEADS * PEER_QDIM), D_MODEL ** -0.5),
        "peer_keys": nrm(ks[19], (DEPTH, 2, N_KEYS, PEER_HALF), PEER_HALF ** -0.5),
        "peer_u": nrm(ks[20], (DEPTH, N_EXPERTS, D_MODEL), D_MODEL ** -0.5),
        "peer_v": nrm(ks[21], (DEPTH, N_EXPERTS, D_MODEL), D_MODEL ** -0.5),
    }


def reference(x, c, ctx, c_ctx, w_ada, b_ada, g_mix_pre, g_mix_post, g_ffn_pre, g_ffn_post,
              w_in, g_q_lora, w_uq, g_kv_lora, w_ukv, w_pool, pool_scale, w_out,
              peer_wq, peer_keys, peer_u, peer_v):
    ROWS = x.shape[1] // GRID_W
    cos, sin = axial_rope_tables(ROWS, x.dtype)
    for layer in range(DEPTH):
        last = layer == DEPTH - 1
        mod = jax.nn.silu(c) @ w_ada[layer] + b_ada[layer]
        sh_m, sc_m, gt_m, sh_f, sc_f, gt_f = jnp.split(mod[:, None, :], 6, axis=-1)
        mod_c = jax.nn.silu(c_ctx) @ w_ada[layer] + b_ada[layer]
        csh_m, csc_m, cgt_m, csh_f, csc_f, cgt_f = jnp.split(mod_c, 6, axis=-1)

        hc = modulate(rmsnorm(ctx, g_mix_pre[layer]), csh_m, csc_m)
        if last:
            kv_in_c = hc @ w_in[layer, :, :KV_IN]
        else:
            proj_c = hc @ w_in[layer]
            kv_in_c = proj_c[..., :KV_IN]
        kc_nope, vc, kc_rope = mla_kv(kv_in_c, g_kv_lora[layer], w_ukv[layer])

        h = modulate(rmsnorm(x, g_mix_pre[layer]), sh_m, sc_m)
        proj = h @ w_in[layer]
        k_nope, v, k_rope = mla_kv(proj[..., :KV_IN], g_kv_lora[layer], w_ukv[layer])
        k_rope = apply_axial_rope(k_rope, cos, sin)
        q_nope, q_rope = mla_q(proj[..., Q_OFF:POOL_OFF], g_q_lora[layer], w_uq[layer])
        q_rope = apply_axial_rope(q_rope, cos[:, None], sin[:, None])
        attn = latent_attention(q_nope, q_rope,
                                jnp.concatenate([kc_nope, k_nope], axis=1),
                                jnp.concatenate([kc_rope, k_rope], axis=1),
                                jnp.concatenate([vc, v], axis=1))
        y = merge_branches(proj, attn, w_pool[layer], pool_scale[layer], w_out[layer])
        x_mid = x + gt_m * rmsnorm(y, g_mix_post[layer])

        hf = modulate(rmsnorm(x_mid, g_ffn_pre[layer]), sh_f, sc_f)
        x_new = x_mid + gt_f * rmsnorm(
            peer_ffn(hf, peer_wq[layer], peer_keys[layer], peer_u[layer], peer_v[layer]),
            g_ffn_post[layer])

        if not last:
            qc_nope, qc_rope = mla_q(proj_c[..., Q_OFF:POOL_OFF], g_q_lora[layer], w_uq[layer])
            attn_c = mla_attend(qc_nope, qc_rope, kc_nope, kc_rope, vc)
            attn_c = attn_c.reshape(attn_c.shape[:2] + (N_HEADS * V_HEAD_DIM,))
            yc = merge_branches(proj_c, attn_c, w_pool[layer], pool_scale[layer], w_out[layer])
            ctx = ctx + cgt_m * rmsnorm(yc, g_mix_post[layer])
            hcf = modulate(rmsnorm(ctx, g_ffn_pre[layer]), csh_f, csc_f)
            ctx = ctx + cgt_f * rmsnorm(
                peer_ffn(hcf, peer_wq[layer], peer_keys[layer], peer_u[layer], peer_v[layer]),
                g_ffn_post[layer])
        x = x_new
    return x
```

```python
import functools

import jax
import jax.numpy as jnp
from jax import lax
from jax.experimental import pallas as pl
from jax.experimental.pallas import tpu as pltpu

F32 = jnp.float32
BF16 = jnp.bfloat16

D_MODEL = 1024
GRID_W = 64
N_HEADS = 8
NOPE = 128
ROPE = 64
VHEAD = 128
QK_HEAD = NOPE + ROPE
Q_LORA = 256
KV_LORA = 128
ROPE_BASE = 10000.0
ATTN_SCALE = QK_HEAD ** -0.5
POOL_GROUPS = 4
POOL_GROUP_DIM = D_MODEL // POOL_GROUPS
POOL_MAX_HALF = 1 << (POOL_GROUPS - 1)
KV_IN = KV_LORA + ROPE
PEER_HEADS = 8
N_KEYS = 128
PEER_HALF = 128
PEER_TOPK = 16
EPS = 1e-6

WIN_SMALL = 512
WIN_POOL = WIN_SMALL
WIN_GATE = WIN_POOL + D_MODEL
WIN_TOTAL = WIN_GATE + 2 * D_MODEL
QH_W = 256

LANE = 128
VMEM_LIMIT = 56 * 1024 * 1024

TS_PROJ = 512
TQ_ATTN = 256
TS_MERGE = 256
TT_PEER = 512
EI_PEER = 8

_NT = (((1,), (1,)), ((), ()))


def _rms(x):
    return x * lax.rsqrt(jnp.mean(x * x, axis=-1, keepdims=True) + EPS)


def _mod_kernel(cc_ref, w_ref, b_ref, o_ref):
    cc = cc_ref[...]
    a = cc * jax.nn.sigmoid(cc)
    o_ref[...] = jnp.dot(a, w_ref[...], preferred_element_type=F32) + b_ref[...]


def _mod_call(cc, w_ada, b_ada):
    rows = cc.shape[0]
    n = w_ada.shape[1]
    bn = 1024
    return pl.pallas_call(
        _mod_kernel,
        grid=(n // bn,),
        in_specs=[pl.BlockSpec((rows, D_MODEL), lambda j: (0, 0)),
                  pl.BlockSpec((D_MODEL, bn), lambda j: (0, j)),
                  pl.BlockSpec((1, bn), lambda j: (0, j))],
        out_specs=pl.BlockSpec((rows, bn), lambda j: (0, j)),
        out_shape=jax.ShapeDtypeStruct((rows, n), F32),
        compiler_params=pltpu.CompilerParams(vmem_limit_bytes=VMEM_LIMIT),
        name="mod",
    )(cc, w_ada, b_ada)


def _ctx_kernel(ctx_ref, mod_ref, gpre_ref, win_ref, gkv_ref, o_ref):
    x = ctx_ref[0]
    m = mod_ref[...]
    h = _rms(x) * gpre_ref[...]
    h = h * (1.0 + m[1:2]) + m[0:1]
    p = jnp.dot(h.astype(BF16), win_ref[...], preferred_element_type=F32)
    ckv_n = _rms(p[:, :KV_LORA]) * gkv_ref[...]
    o_ref[0, :, 0:KV_LORA] = ckv_n.astype(BF16)
    o_ref[0, :, KV_LORA:KV_IN] = p[:, KV_LORA:KV_IN].astype(BF16)


def _ctx_call(ctx, mod_c, g_pre, win_r, g_kv):
    b, lc, _ = ctx.shape
    return pl.pallas_call(
        _ctx_kernel,
        grid=(b,),
        in_specs=[pl.BlockSpec((1, lc, D_MODEL), lambda i: (i, 0, 0)),
                  pl.BlockSpec((6, D_MODEL), lambda i: (0, 0)),
                  pl.BlockSpec((1, D_MODEL), lambda i: (0, 0)),
                  pl.BlockSpec((D_MODEL, 256), lambda i: (0, 0)),
                  pl.BlockSpec((1, KV_LORA), lambda i: (0, 0))],
        out_specs=pl.BlockSpec((1, lc, KV_IN), lambda i: (i, 0, 0)),
        out_shape=jax.ShapeDtypeStruct((b, lc, KV_IN), BF16),
        compiler_params=pltpu.CompilerParams(vmem_limit_bytes=VMEM_LIMIT),
        name="ctx_kv",
    )(ctx, mod_c, g_pre, win_r, g_kv)


def _proj_kernel(x_ref, mod_ref, gpre_ref, win_ref, gq_ref, wuq_ref, gkv_ref, wk_ref, cos_ref, sin_ref,
                 kcat_ref, q_ref, pool_ref, gates_ref):
    x = x_ref[0]
    m = mod_ref[0]
    h = _rms(x) * gpre_ref[...]
    hb = (h * (1.0 + m[1:2]) + m[0:1]).astype(BF16)
    cos = cos_ref[...]
    sin = sin_ref[...]

    p0 = jnp.dot(hb, win_ref[:, 0:WIN_SMALL], preferred_element_type=F32)
    ckv_n = _rms(p0[:, 0:KV_LORA]) * gkv_ref[...]
    k_rot = p0[:, 128:192] * cos + p0[:, 192:256] * sin
    kcat_ref[0, :, 0:KV_LORA] = ckv_n.astype(BF16)
    kcat_ref[0, :, KV_LORA:KV_IN] = k_rot.astype(BF16)

    cqn = (_rms(p0[:, 256:512]) * gq_ref[...]).astype(BF16)
    for hd in range(N_HEADS):
        qh = jnp.dot(cqn, wuq_ref[:, hd * QH_W:(hd + 1) * QH_W], preferred_element_type=F32)
        q_abs = jnp.dot(qh[:, 0:NOPE].astype(BF16), wk_ref[hd], preferred_element_type=F32)
        q_rot = qh[:, 128:192] * cos + qh[:, 192:256] * sin
        q_ref[0, hd, :, 0:KV_LORA] = (q_abs * ATTN_SCALE).astype(BF16)
        q_ref[0, hd, :, KV_LORA:KV_IN] = (q_rot * ATTN_SCALE).astype(BF16)

    pool_ref[0] = jnp.dot(hb, win_ref[:, WIN_POOL:WIN_GATE], preferred_element_type=F32).astype(BF16)
    for half in range(2):
        lo = WIN_GATE + half * D_MODEL
        g = jnp.dot(hb, win_ref[:, lo:lo + D_MODEL], preferred_element_type=F32)
        gates_ref[0, :, half * D_MODEL:(half + 1) * D_MODEL] = jax.nn.sigmoid(g).astype(BF16)


def _proj_call(x, mod, g_pre, win_r, g_q, wuq_r, g_kv, wk_abs, cos_t, sin_t):
    b, s, _ = x.shape
    ts = min(TS_PROJ, s)
    const2 = lambda i, j: (0, 0)
    return pl.pallas_call(
        _proj_kernel,
        grid=(b, s // ts),
        in_specs=[pl.BlockSpec((1, ts, D_MODEL), lambda i, j: (i, j, 0)),
                  pl.BlockSpec((1, 6, D_MODEL), lambda i, j: (i, 0, 0)),
                  pl.BlockSpec((1, D_MODEL), const2),
                  pl.BlockSpec((D_MODEL, WIN_TOTAL), const2),
                  pl.BlockSpec((1, Q_LORA), const2),
                  pl.BlockSpec((Q_LORA, N_HEADS * QH_W), const2),
                  pl.BlockSpec((1, KV_LORA), const2),
                  pl.BlockSpec((N_HEADS, NOPE, KV_LORA), lambda i, j: (0, 0, 0)),
                  pl.BlockSpec((ts, ROPE), lambda i, j: (j, 0)),
                  pl.BlockSpec((ts, ROPE), lambda i, j: (j, 0))],
        out_specs=[pl.BlockSpec((1, ts, KV_IN), lambda i, j: (i, j, 0)),
                   pl.BlockSpec((1, N_HEADS, ts, KV_IN), lambda i, j: (i, 0, j, 0)),
                   pl.BlockSpec((1, ts, D_MODEL), lambda i, j: (i, j, 0)),
                   pl.BlockSpec((1, ts, 2 * D_MODEL), lambda i, j: (i, j, 0))],
        out_shape=[jax.ShapeDtypeStruct((b, s, KV_IN), BF16),
                   jax.ShapeDtypeStruct((b, N_HEADS, s, KV_IN), BF16),
                   jax.ShapeDtypeStruct((b, s, D_MODEL), BF16),
                   jax.ShapeDtypeStruct((b, s, 2 * D_MODEL), BF16)],
        compiler_params=pltpu.CompilerParams(vmem_limit_bytes=VMEM_LIMIT),
        name="proj",
    )(x, mod, g_pre, win_r, g_q, wuq_r, g_kv, wk_abs, cos_t, sin_t)


def _attn_kernel(q_ref, k_ref, wv_ref, o_ref):
    k = k_ref[0]
    v = k[:, 0:KV_LORA]
    for hd in range(N_HEADS):
        s = lax.dot_general(q_ref[0, hd], k, _NT, preferred_element_type=F32)
        p = jnp.exp(s - jnp.max(s, axis=-1, keepdims=True))
        l = jnp.sum(p, axis=-1, keepdims=True)
        o = jnp.dot(p.astype(BF16), v, preferred_element_type=F32) / l
        a = jnp.dot(o.astype(BF16), wv_ref[hd], preferred_element_type=F32)
        o_ref[0, :, hd * VHEAD:(hd + 1) * VHEAD] = a.astype(BF16)


def _attn_call(q, kall, wv):
    b, _, s, _ = q.shape
    lk = kall.shape[1]
    tq = min(TQ_ATTN, s)
    return pl.pallas_call(
        _attn_kernel,
        grid=(b, s // tq),
        in_specs=[pl.BlockSpec((1, N_HEADS, tq, KV_IN), lambda i, j: (i, 0, j, 0)),
                  pl.BlockSpec((1, lk, KV_IN), lambda i, j: (i, 0, 0)),
                  pl.BlockSpec((N_HEADS, KV_LORA, VHEAD), lambda i, j: (0, 0, 0))],
        out_specs=pl.BlockSpec((1, tq, D_MODEL), lambda i, j: (i, j, 0)),
        out_shape=jax.ShapeDtypeStruct((b, s, D_MODEL), BF16),
        compiler_params=pltpu.CompilerParams(vmem_limit_bytes=VMEM_LIMIT),
        name="attn",
    )(q, kall, wv)


def _pool_kernel(xp_ref, w_ref, ps_ref, o_ref):
    g = pl.program_id(1)
    x = xp_ref[0].astype(F32)
    n = x.shape[0]
    t = lax.broadcasted_iota(jnp.int32, x.shape, 0)
    pad = POOL_MAX_HALF
    zeros = jnp.zeros((pad, x.shape[1]), F32)
    xe = jnp.concatenate([zeros, x, zeros], axis=0)
    ne = n + 2 * pad

    def down(a, k):
        return pltpu.roll(a, k, 0)

    def up(a, k):
        return pltpu.roll(a, ne - k, 0)

    w = down(xe, 1) + xe
    for lvl in range(1, POOL_GROUPS):
        k = 1 << (lvl - 1)
        w = jnp.where(g >= lvl, down(w, k) + up(w, k), w)
    w = w[pad:pad + n]
    half = lax.shift_left(jnp.int32(1), g)
    count = (jnp.minimum(t + half, n) - jnp.maximum(t - half, 0)).astype(F32)
    pooled = (w / count - x).astype(BF16)
    y = jnp.dot(pooled, w_ref[0], preferred_element_type=F32) * ps_ref[...]
    o_ref[0] = y.astype(BF16)


def _pool_call(pool_in, w_pool, pool_scale):
    b, s, _ = pool_in.shape
    gd = POOL_GROUP_DIM
    return pl.pallas_call(
        _pool_kernel,
        grid=(b, POOL_GROUPS),
        in_specs=[pl.BlockSpec((1, s, gd), lambda i, g: (i, 0, g)),
                  pl.BlockSpec((1, gd, gd), lambda i, g: (g, 0, 0)),
                  pl.BlockSpec((1, gd), lambda i, g: (0, g))],
        out_specs=pl.BlockSpec((1, s, gd), lambda i, g: (i, 0, g)),
        out_shape=jax.ShapeDtypeStruct((b, s, D_MODEL), BF16),
        compiler_params=pltpu.CompilerParams(vmem_limit_bytes=VMEM_LIMIT),
        name="pool",
    )(pool_in, w_pool, pool_scale)


def _top16(s, rows16):
    out = jnp.full((PEER_TOPK, s.shape[1]), -jnp.inf, F32)
    for k in range(PEER_TOPK):
        m = jnp.max(s, axis=0, keepdims=True)
        out = jnp.where(rows16 == k, m, out)
        s = jnp.where(s == m, -jnp.inf, s)
    return out


def _peer_stats(s1, s2):
    lanes = s1.shape[1]
    rows16 = lax.broadcasted_iota(jnp.int32, (PEER_TOPK, lanes), 0)
    rows8 = lax.broadcasted_iota(jnp.int32, (8, lanes), 0)
    a = _top16(s1, rows16)
    b = _top16(s2, rows16)
    pieces = [a + b[0:1]]
    pieces += [a[0:8] + b[c:c + 1] for c in range(1, 8)]
    pieces += [b[8:16] + a[0:1]]
    cand = jnp.concatenate(pieces, axis=0)
    work = cand
    thr = None
    for _ in range(PEER_TOPK):
        thr = jnp.max(work, axis=0, keepdims=True)
        work = jnp.where(work == thr, -jnp.inf, work)
    top = cand[0:1]
    sel = cand >= thr
    z = jnp.sum(jnp.where(sel, jnp.exp(cand - top), 0.0), axis=0, keepdims=True)
    inf = jnp.inf
    cv_hi = jnp.where(sel[8:16], b[0:1], inf)
    cv_lo = jnp.where(sel[0:8], b[0:1], inf)
    for c in range(1, 8):
        cv_lo = jnp.minimum(cv_lo, jnp.where(sel[8 + 8 * c:16 + 8 * c], b[c:c + 1], inf))
    extra = jnp.min(jnp.where(sel[72:80], b[8:16], inf), axis=0, keepdims=True)
    cv_lo = jnp.minimum(cv_lo, jnp.where(rows8 == 0, extra, inf))
    cut = jnp.full(s1.shape, inf, F32)
    for r in range(PEER_TOPK):
        cv = cv_lo[r:r + 1] if r < 8 else cv_hi[r - 8:r - 7]
        cut = jnp.where(s1 == a[r:r + 1], cv, cut)
    e1 = jnp.exp(s1 - a[0:1]) / z
    e2 = jnp.exp(s2 - b[0:1])
    return cut, e1, e2


def _merge_kernel(x_ref, attn_ref, yp_ref, gates_ref, mod_ref, wout_ref, gpost_ref, gffn_ref, wq_ref, keys_ref,
                  xmid_ref, hf_ref, s2_ref, e2_ref, cut_ref, e1_ref):
    m = mod_ref[0]
    ga = gates_ref[0, :, 0:D_MODEL].astype(F32)
    gb = gates_ref[0, :, D_MODEL:2 * D_MODEL].astype(F32)
    merged = ga * attn_ref[0].astype(F32) + gb * yp_ref[0].astype(F32)
    y = jnp.dot(merged.astype(BF16), wout_ref[...], preferred_element_type=F32)
    x_mid = x_ref[0] + m[2:3] * (_rms(y) * gpost_ref[...])
    xmid_ref[0] = x_mid
    hf = (_rms(x_mid) * gffn_ref[...]) * (1.0 + m[4:5]) + m[3:4]
    hfb = hf.astype(BF16)
    hf_ref[0] = hfb
    q = jnp.dot(hfb, wq_ref[...], preferred_element_type=F32).astype(BF16)
    k1 = keys_ref[0]
    k2 = keys_ref[1]
    ts = q.shape[0]
    for hd in range(PEER_HEADS):
        base = hd * 2 * PEER_HALF
        s1 = lax.dot_general(k1, q[:, base:base + PEER_HALF], _NT, preferred_element_type=F32)
        s2 = lax.dot_general(k2, q[:, base + PEER_HALF:base + 2 * PEER_HALF], _NT, preferred_element_type=F32)
        for lt in range(ts // LANE):
            sl = slice(lt * LANE, (lt + 1) * LANE)
            cut, e1, e2 = _peer_stats(s1[:, sl], s2[:, sl])
            s2_ref[hd, :, sl] = s2[:, sl]
            e2_ref[hd, :, sl] = e2
            cut_ref[:, hd, sl] = cut
            e1_ref[:, hd, sl] = e1


def _merge_call(x, attn, ypool, gates, mod, w_out, g_post, g_ffn, wq, keys):
    b, s, _ = x.shape
    ts = min(TS_MERGE, s)
    nt = s // ts
    t_all = b * s
    const2 = lambda i, j: (0, 0)
    tok3 = lambda i, j: (i, j, 0)
    lane_t = lambda i, j: (0, 0, i * nt + j)
    return pl.pallas_call(
        _merge_kernel,
        grid=(b, nt),
        in_specs=[pl.BlockSpec((1, ts, D_MODEL), tok3),
                  pl.BlockSpec((1, ts, D_MODEL), tok3),
                  pl.BlockSpec((1, ts, D_MODEL), tok3),
                  pl.BlockSpec((1, ts, 2 * D_MODEL), tok3),
                  pl.BlockSpec((1, 6, D_MODEL), lambda i, j: (i, 0, 0)),
                  pl.BlockSpec((D_MODEL, D_MODEL), const2),
                  pl.BlockSpec((1, D_MODEL), const2),
                  pl.BlockSpec((1, D_MODEL), const2),
                  pl.BlockSpec((D_MODEL, PEER_HEADS * 2 * PEER_HALF), const2),
                  pl.BlockSpec((2, N_KEYS, PEER_HALF), lambda i, j: (0, 0, 0))],
        out_specs=[pl.BlockSpec((1, ts, D_MODEL), tok3),
                   pl.BlockSpec((1, ts, D_MODEL), tok3),
                   pl.BlockSpec((PEER_HEADS, N_KEYS, ts), lane_t),
                   pl.BlockSpec((PEER_HEADS, N_KEYS, ts), lane_t),
                   pl.BlockSpec((N_KEYS, PEER_HEADS, ts), lane_t),
                   pl.BlockSpec((N_KEYS, PEER_HEADS, ts), lane_t)],
        out_shape=[jax.ShapeDtypeStruct((b, s, D_MODEL), F32),
                   jax.ShapeDtypeStruct((b, s, D_MODEL), BF16),
                   jax.ShapeDtypeStruct((PEER_HEADS, N_KEYS, t_all), F32),
                   jax.ShapeDtypeStruct((PEER_HEADS, N_KEYS, t_all), F32),
                   jax.ShapeDtypeStruct((N_KEYS, PEER_HEADS, t_all), F32),
                   jax.ShapeDtypeStruct((N_KEYS, PEER_HEADS, t_all), F32)],
        compiler_params=pltpu.CompilerParams(vmem_limit_bytes=VMEM_LIMIT),
        name="merge",
    )(x, attn, ypool, gates, mod, w_out, g_post, g_ffn, wq, keys)


def _peer_kernel(hf_ref, u_ref, vt_ref, s2_ref, e2_ref, cut_ref, e1_ref, xmid_ref, mod_ref, gpost_ref,
                 o_ref, acc_ref, act_ref, a_ref):
    e = pl.program_id(2)
    tt = hf_ref.shape[1]
    n_lt = tt // LANE

    @pl.when(e == 0)
    def _():
        acc_ref[...] = jnp.zeros_like(acc_ref)

    act_ref[...] = lax.dot_general(u_ref[...], hf_ref[0], _NT, preferred_element_type=F32)

    def body(it, carry):
        i = it // n_lt
        lt = it % n_lt
        lanes = pl.ds(pl.multiple_of(lt * LANE, LANE), LANE)
        cut_all = cut_ref[i, :, lanes]
        e1_all = e1_ref[i, :, lanes]
        for jc in range(2):
            w = jnp.zeros((64, LANE), F32)
            for hd in range(PEER_HEADS):
                s2 = s2_ref[hd, jc * 64:(jc + 1) * 64, lanes]
                e2 = e2_ref[hd, jc * 64:(jc + 1) * 64, lanes]
                w = w + jnp.where(s2 >= cut_all[hd:hd + 1], e2, 0.0) * e1_all[hd:hd + 1]
            rows = pl.ds(pl.multiple_of(i * N_KEYS + jc * 64, 64), 64)
            a = act_ref[rows, lanes]
            gelu = 0.5 * a * (1.0 + lax.erf(a * 0.7071067811865476))
            a_ref[rows, lanes] = (w * gelu).astype(BF16)
        return carry

    lax.fori_loop(0, EI_PEER * n_lt, body, 0)

    acc_ref[...] += jnp.dot(vt_ref[...], a_ref[...], preferred_element_type=F32)

    @pl.when(e == pl.num_programs(2) - 1)
    def _():
        m = mod_ref[0]
        y = acc_ref[...].T
        o_ref[0] = xmid_ref[0] + m[5:6] * (_rms(y) * gpost_ref[...])


def _peer_call(hf, u_b, vt_b, s2t, e2t, cutt, e1t, x_mid, mod, g_post):
    b, s, _ = hf.shape
    tt = min(TT_PEER, s)
    nt = s // tt
    et = EI_PEER * N_KEYS
    ne = N_KEYS // EI_PEER
    tok3 = lambda i, j, e: (i, j, 0)
    lane_t = lambda i, j, e: (0, 0, i * nt + j)
    return pl.pallas_call(
        _peer_kernel,
        grid=(b, nt, ne),
        in_specs=[pl.BlockSpec((1, tt, D_MODEL), tok3),
                  pl.BlockSpec((et, D_MODEL), lambda i, j, e: (e, 0)),
                  pl.BlockSpec((D_MODEL, et), lambda i, j, e: (0, e)),
                  pl.BlockSpec((PEER_HEADS, N_KEYS, tt), lane_t),
                  pl.BlockSpec((PEER_HEADS, N_KEYS, tt), lane_t),
                  pl.BlockSpec((EI_PEER, PEER_HEADS, tt), lambda i, j, e: (e, 0, i * nt + j)),
                  pl.BlockSpec((EI_PEER, PEER_HEADS, tt), lambda i, j, e: (e, 0, i * nt + j)),
                  pl.BlockSpec((1, tt, D_MODEL), tok3),
                  pl.BlockSpec((1, 6, D_MODEL), lambda i, j, e: (i, 0, 0)),
                  pl.BlockSpec((1, D_MODEL), lambda i, j, e: (0, 0))],
        out_specs=pl.BlockSpec((1, tt, D_MODEL), tok3),
        out_shape=jax.ShapeDtypeStruct((b, s, D_MODEL), F32),
        scratch_shapes=[pltpu.VMEM((D_MODEL, tt), F32),
                        pltpu.VMEM((et, tt), F32),
                        pltpu.VMEM((et, tt), BF16)],
        compiler_params=pltpu.CompilerParams(
            vmem_limit_bytes=VMEM_LIMIT,
            dimension_semantics=("arbitrary", "arbitrary", "arbitrary")),
        name="peer",
    )(hf, u_b, vt_b, s2t, e2t, cutt, e1t, x_mid, mod, g_post)


def _rope_tables(seq):
    rows = seq // GRID_W
    row = jnp.broadcast_to(jnp.arange(rows, dtype=F32)[:, None], (rows, GRID_W)).reshape(-1)
    col = jnp.broadcast_to(jnp.arange(GRID_W, dtype=F32)[None, :], (rows, GRID_W)).reshape(-1)
    n_freq = ROPE // 4
    inv_freq = ROPE_BASE ** (-jnp.arange(n_freq, dtype=F32) / n_freq)
    ar = row[:, None] * inv_freq
    ac = col[:, None] * inv_freq
    cos = jnp.concatenate([jnp.cos(ar), jnp.cos(ar), jnp.cos(ac), jnp.cos(ac)], axis=1)
    sin = jnp.concatenate([-jnp.sin(ar), jnp.sin(ar), -jnp.sin(ac), jnp.sin(ac)], axis=1)
    return cos, sin


def _swap_halves(w):
    q = ROPE // 4
    return jnp.concatenate([w[..., q:2 * q], w[..., 0:q], w[..., 3 * q:4 * q], w[..., 2 * q:3 * q]], axis=-1)


def kernel(x, c, ctx, c_ctx, w_ada, b_ada, g_mix_pre, g_mix_post, g_ffn_pre, g_ffn_post, w_in, g_q_lora, w_uq,
           g_kv_lora, w_ukv, w_pool, pool_scale, w_out, peer_wq, peer_keys, peer_u, peer_v):
    assert w_ada.shape[0] == 1, "single-layer block"
    b, s, _ = x.shape
    layer = 0

    w_in_l = w_in[layer]
    k_rope_w = w_in_l[:, KV_LORA:KV_IN]
    win_r = jnp.concatenate([w_in_l[:, 0:KV_IN], _swap_halves(k_rope_w), w_in_l[:, KV_IN:]], axis=1).astype(BF16)
    wuq_h = w_uq[layer].reshape(Q_LORA, N_HEADS, QK_HEAD)
    wuq_r = jnp.concatenate([wuq_h, _swap_halves(wuq_h[..., NOPE:])], axis=-1).reshape(Q_LORA, N_HEADS * QH_W)
    wuq_r = wuq_r.astype(BF16)
    wukv_h = w_ukv[layer].reshape(KV_LORA, N_HEADS, NOPE + VHEAD)
    wk_abs = jnp.transpose(wukv_h[..., 0:NOPE], (1, 2, 0)).astype(BF16)
    wv_up = jnp.transpose(wukv_h[..., NOPE:], (1, 0, 2)).astype(BF16)
    cos_t, sin_t = _rope_tables(s)
    u_b = peer_u[layer].astype(BF16)
    vt_b = peer_v[layer].T.astype(BF16)
    row = lambda v: v[layer][None, :]

    n_rows = -(-(b + 1) // 8) * 8
    cc = jnp.concatenate([c, c_ctx[None, :], jnp.zeros((n_rows - b - 1, D_MODEL), F32)], axis=0)
    mod_all = _mod_call(cc, w_ada[layer], b_ada[layer][None, :])
    mod = mod_all[:b].reshape(b, 6, D_MODEL)
    mod_c = mod_all[b].reshape(6, D_MODEL)

    k_ctx = _ctx_call(ctx, mod_c, row(g_mix_pre), win_r[:, 0:256], row(g_kv_lora))
    k_lat, q, pool_in, gates = _proj_call(x, mod, row(g_mix_pre), win_r, row(g_q_lora), wuq_r, row(g_kv_lora),
                                          wk_abs, cos_t, sin_t)
    k_all = jnp.concatenate([k_ctx, k_lat], axis=1)
    attn = _attn_call(q, k_all, wv_up)
    ypool = _pool_call(pool_in, w_pool[layer].astype(BF16), row(pool_scale))
    x_mid, hf, s2t, e2t, cutt, e1t = _merge_call(
        x, attn, ypool, gates, mod, w_out[layer].astype(BF16), row(g_mix_post), row(g_ffn_pre),
        peer_wq[layer].astype(BF16), peer_keys[layer].astype(BF16))
    return _peer_call(hf, u_b, vt_b, s2t, e2t, cutt, e1t, x_mid, mod, row(g_ffn_post))
```

```python
import functools

import jax
import jax.numpy as jnp
from jax import lax
from jax.experimental import pallas as pl
from jax.experimental.pallas import tpu as pltpu

F32 = jnp.float32
BF16 = jnp.bfloat16

D_MODEL = 1024
GRID_W = 64
N_HEADS = 8
NOPE = 128
ROPE = 64
VHEAD = 128
QK_HEAD = NOPE + ROPE
Q_LORA = 256
KV_LORA = 128
ROPE_BASE = 10000.0
ATTN_SCALE = QK_HEAD ** -0.5
POOL_GROUPS = 4
POOL_GROUP_DIM = D_MODEL // POOL_GROUPS
POOL_MAX_HALF = 1 << (POOL_GROUPS - 1)
KV_IN = KV_LORA + ROPE
PEER_HEADS = 8
N_KEYS = 128
PEER_HALF = 128
PEER_TOPK = 16
EPS = 1e-6

WIN_SMALL = 512
WIN_POOL = WIN_SMALL
WIN_GATE = WIN_POOL + D_MODEL
WIN_TOTAL = WIN_GATE + 2 * D_MODEL
QH_W = 256

LANE = 128
VMEM_LIMIT = 56 * 1024 * 1024

TS_PROJ = 512
TQ_ATTN = 256
TS_MERGE = 256
TT_PEER = 512
EI_PEER = 8

_NT = (((1,), (1,)), ((), ()))


def _rms(x):
    return x * lax.rsqrt(jnp.mean(x * x, axis=-1, keepdims=True) + EPS)


def _mod_kernel(cc_ref, w_ref, b_ref, o_ref):
    cc = cc_ref[...]
    a = cc * jax.nn.sigmoid(cc)
    o_ref[...] = jnp.dot(a, w_ref[...], preferred_element_type=F32) + b_ref[...]


def _mod_call(cc, w_ada, b_ada):
    rows = cc.shape[0]
    n = w_ada.shape[1]
    bn = 1024
    return pl.pallas_call(
        _mod_kernel,
        grid=(n // bn,),
        in_specs=[pl.BlockSpec((rows, D_MODEL), lambda j: (0, 0)),
                  pl.BlockSpec((D_MODEL, bn), lambda j: (0, j)),
                  pl.BlockSpec((1, bn), lambda j: (0, j))],
        out_specs=pl.BlockSpec((rows, bn), lambda j: (0, j)),
        out_shape=jax.ShapeDtypeStruct((rows, n), F32),
        compiler_params=pltpu.CompilerParams(vmem_limit_bytes=VMEM_LIMIT),
        name="mod",
    )(cc, w_ada, b_ada)


def _ctx_kernel(ctx_ref, mod_ref, gpre_ref, win_ref, gkv_ref, o_ref):
    x = ctx_ref[0]
    m = mod_ref[...]
    h = _rms(x) * gpre_ref[...]
    h = h * (1.0 + m[1:2]) + m[0:1]
    p = jnp.dot(h.astype(BF16), win_ref[...], preferred_element_type=F32)
    ckv_n = _rms(p[:, :KV_LORA]) * gkv_ref[...]
    o_ref[0, :, 0:KV_LORA] = ckv_n.astype(BF16)
    o_ref[0, :, KV_LORA:KV_IN] = p[:, KV_LORA:KV_IN].astype(BF16)


def _ctx_call(ctx, mod_c, g_pre, win_r, g_kv):
    b, lc, _ = ctx.shape
    return pl.pallas_call(
        _ctx_kernel,
        grid=(b,),
        in_specs=[pl.BlockSpec((1, lc, D_MODEL), lambda i: (i, 0, 0)),
                  pl.BlockSpec((6, D_MODEL), lambda i: (0, 0)),
                  pl.BlockSpec((1, D_MODEL), lambda i: (0, 0)),
                  pl.BlockSpec((D_MODEL, 256), lambda i: (0, 0)),
                  pl.BlockSpec((1, KV_LORA), lambda i: (0, 0))],
        out_specs=pl.BlockSpec((1, lc, KV_IN), lambda i: (i, 0, 0)),
        out_shape=jax.ShapeDtypeStruct((b, lc, KV_IN), BF16),
        compiler_params=pltpu.CompilerParams(vmem_limit_bytes=VMEM_LIMIT),
        name="ctx_kv",
    )(ctx, mod_c, g_pre, win_r, g_kv)


def _proj_kernel(x_ref, mod_ref, gpre_ref, win_ref, gq_ref, wuq_ref, gkv_ref, wk_ref, cos_ref, sin_ref,
                 kcat_ref, q_ref, pool_ref, gates_ref):
    x = x_ref[0]
    m = mod_ref[0]
    h = _rms(x) * gpre_ref[...]
    hb = (h * (1.0 + m[1:2]) + m[0:1]).astype(BF16)
    cos = cos_ref[...]
    sin = sin_ref[...]

    p0 = jnp.dot(hb, win_ref[:, 0:WIN_SMALL], preferred_element_type=F32)
    ckv_n = _rms(p0[:, 0:KV_LORA]) * gkv_ref[...]
    k_rot = p0[:, 128:192] * cos + p0[:, 192:256] * sin
    kcat_ref[0, :, 0:KV_LORA] = ckv_n.astype(BF16)
    kcat_ref[0, :, KV_LORA:KV_IN] = k_rot.astype(BF16)

    cqn = (_rms(p0[:, 256:512]) * gq_ref[...]).astype(BF16)
    for hd in range(N_HEADS):
        qh = jnp.dot(cqn, wuq_ref[:, hd * QH_W:(hd + 1) * QH_W], preferred_element_type=F32)
        q_abs = jnp.dot(qh[:, 0:NOPE].astype(BF16), wk_ref[hd], preferred_element_type=F32)
        q_rot = qh[:, 128:192] * cos + qh[:, 192:256] * sin
        q_ref[0, hd, :, 0:KV_LORA] = (q_abs * ATTN_SCALE).astype(BF16)
        q_ref[0, hd, :, KV_LORA:KV_IN] = (q_rot * ATTN_SCALE).astype(BF16)

    pool_ref[0] = jnp.dot(hb, win_ref[:, WIN_POOL:WIN_GATE], preferred_element_type=F32).astype(BF16)
    for half in range(2):
        lo = WIN_GATE + half * D_MODEL
        g = jnp.dot(hb, win_ref[:, lo:lo + D_MODEL], preferred_element_type=F32)
        gates_ref[0, :, half * D_MODEL:(half + 1) * D_MODEL] = jax.nn.sigmoid(g).astype(BF16)


def _proj_call(x, mod, g_pre, win_r, g_q, wuq_r, g_kv, wk_abs, cos_t, sin_t):
    b, s, _ = x.shape
    ts = min(TS_PROJ, s)
    const2 = lambda i, j: (0, 0)
    return pl.pallas_call(
        _proj_kernel,
        grid=(b, s // ts),
        in_specs=[pl.BlockSpec((1, ts, D_MODEL), lambda i, j: (i, j, 0)),
                  pl.BlockSpec((1, 6, D_MODEL), lambda i, j: (i, 0, 0)),
                  pl.BlockSpec((1, D_MODEL), const2),
                  pl.BlockSpec((D_MODEL, WIN_TOTAL), const2),
                  pl.BlockSpec((1, Q_LORA), const2),
                  pl.BlockSpec((Q_LORA, N_HEADS * QH_W), const2),
                  pl.BlockSpec((1, KV_LORA), const2),
                  pl.BlockSpec((N_HEADS, NOPE, KV_LORA), lambda i, j: (0, 0, 0)),
                  pl.BlockSpec((ts, ROPE), lambda i, j: (j, 0)),
                  pl.BlockSpec((ts, ROPE), lambda i, j: (j, 0))],
        out_specs=[pl.BlockSpec((1, ts, KV_IN), lambda i, j: (i, j, 0)),
                   pl.BlockSpec((1, N_HEADS, ts, KV_IN), lambda i, j: (i, 0, j, 0)),
                   pl.BlockSpec((1, ts, D_MODEL), lambda i, j: (i, j, 0)),
                   pl.BlockSpec((1, ts, 2 * D_MODEL), lambda i, j: (i, j, 0))],
        out_shape=[jax.ShapeDtypeStruct((b, s, KV_IN), BF16),
                   jax.ShapeDtypeStruct((b, N_HEADS, s, KV_IN), BF16),
                   jax.ShapeDtypeStruct((b, s, D_MODEL), BF16),
                   jax.ShapeDtypeStruct((b, s, 2 * D_MODEL), BF16)],
        compiler_params=pltpu.CompilerParams(vmem_limit_bytes=VMEM_LIMIT),
        name="proj",
    )(x, mod, g_pre, win_r, g_q, wuq_r, g_kv, wk_abs, cos_t, sin_t)


def _attn_kernel(q_ref, k_ref, wv_ref, o_ref):
    k = k_ref[0]
    v = k[:, 0:KV_LORA]
    for hd in range(N_HEADS):
        s = lax.dot_general(q_ref[0, hd], k, _NT, preferred_element_type=F32)
        p = jnp.exp(s - jnp.max(s, axis=-1, keepdims=True))
        l = jnp.sum(p, axis=-1, keepdims=True)
        o = jnp.dot(p.astype(BF16), v, preferred_element_type=F32) / l
        a = jnp.dot(o.astype(BF16), wv_ref[hd], preferred_element_type=F32)
        o_ref[0, :, hd * VHEAD:(hd + 1) * VHEAD] = a.astype(BF16)


def _attn_call(q, kall, wv):
    b, _, s, _ = q.shape
    lk = kall.shape[1]
    tq = min(TQ_ATTN, s)
    return pl.pallas_call(
        _attn_kernel,
        grid=(b, s // tq),
        in_specs=[pl.BlockSpec((1, N_HEADS, tq, KV_IN), lambda i, j: (i, 0, j, 0)),
                  pl.BlockSpec((1, lk, KV_IN), lambda i, j: (i, 0, 0)),
                  pl.BlockSpec((N_HEADS, KV_LORA, VHEAD), lambda i, j: (0, 0, 0))],
        out_specs=pl.BlockSpec((1, tq, D_MODEL), lambda i, j: (i, j, 0)),
        out_shape=jax.ShapeDtypeStruct((b, s, D_MODEL), BF16),
        compiler_params=pltpu.CompilerParams(vmem_limit_bytes=VMEM_LIMIT),
        name="attn",
    )(q, kall, wv)


def _pool_kernel(xp_ref, w_ref, ps_ref, o_ref):
    g = pl.program_id(1)
    x = xp_ref[0].astype(F32)
    n = x.shape[0]
    t = lax.broadcasted_iota(jnp.int32, x.shape, 0)
    pad = POOL_MAX_HALF
    zeros = jnp.zeros((pad, x.shape[1]), F32)
    xe = jnp.concatenate([zeros, x, zeros], axis=0)
    ne = n + 2 * pad

    def down(a, k):
        return pltpu.roll(a, k, 0)

    def up(a, k):
        return pltpu.roll(a, ne - k, 0)

    w = down(xe, 1) + xe
    for lvl in range(1, POOL_GROUPS):
        k = 1 << (lvl - 1)
        w = jnp.where(g >= lvl, down(w, k) + up(w, k), w)
    w = w[pad:pad + n]
    half = lax.shift_left(jnp.int32(1), g)
    count = (jnp.minimum(t + half, n) - jnp.maximum(t - half, 0)).astype(F32)
    pooled = (w / count - x).astype(BF16)
    y = jnp.dot(pooled, w_ref[0], preferred_element_type=F32) * ps_ref[...]
    o_ref[0] = y.astype(BF16)


def _pool_call(pool_in, w_pool, pool_scale):
    b, s, _ = pool_in.shape
    gd = POOL_GROUP_DIM
    return pl.pallas_call(
        _pool_kernel,
        grid=(b, POOL_GROUPS),
        in_specs=[pl.BlockSpec((1, s, gd), lambda i, g: (i, 0, g)),
                  pl.BlockSpec((1, gd, gd), lambda i, g: (g, 0, 0)),
                  pl.BlockSpec((1, gd), lambda i, g: (0, g))],
        out_specs=pl.BlockSpec((1, s, gd), lambda i, g: (i, 0, g)),
        out_shape=jax.ShapeDtypeStruct((b, s, D_MODEL), BF16),
        compiler_params=pltpu.CompilerParams(vmem_limit_bytes=VMEM_LIMIT),
        name="pool",
    )(pool_in, w_pool, pool_scale)


def _top16(s, rows16):
    out = jnp.full((PEER_TOPK, s.shape[1]), -jnp.inf, F32)
    for k in range(PEER_TOPK):
        m = jnp.max(s, axis=0, keepdims=True)
        out = jnp.where(rows16 == k, m, out)
        s = jnp.where(s == m, -jnp.inf, s)
    return out


def _peer_stats(s1, s2):
    lanes = s1.shape[1]
    rows16 = lax.broadcasted_iota(jnp.int32, (PEER_TOPK, lanes), 0)
    rows8 = lax.broadcasted_iota(jnp.int32, (8, lanes), 0)
    a = _top16(s1, rows16)
    b = _top16(s2, rows16)
    pieces = [a + b[0:1]]
    pieces += [a[0:8] + b[c:c + 1] for c in range(1, 8)]
    pieces += [b[8:16] + a[0:1]]
    cand = jnp.concatenate(pieces, axis=0)
    work = cand
    thr = None
    for _ in range(PEER_TOPK):
        thr = jnp.max(work, axis=0, keepdims=True)
        work = jnp.where(work == thr, -jnp.inf, work)
    top = cand[0:1]
    sel = cand >= thr
    z = jnp.sum(jnp.where(sel, jnp.exp(cand - top), 0.0), axis=0, keepdims=True)
    inf = jnp.inf
    cv_hi = jnp.where(sel[8:16], b[0:1], inf)
    cv_lo = jnp.where(sel[0:8], b[0:1], inf)
    for c in range(1, 8):
        cv_lo = jnp.minimum(cv_lo, jnp.where(sel[8 + 8 * c:16 + 8 * c], b[c:c + 1], inf))
    extra = jnp.min(jnp.where(sel[72:80], b[8:16], inf), axis=0, keepdims=True)
    cv_lo = jnp.minimum(cv_lo, jnp.where(rows8 == 0, extra, inf))
    cut = jnp.full(s1.shape, inf, F32)
    for r in range(PEER_TOPK):
        cv = cv_lo[r:r + 1] if r < 8 else cv_hi[r - 8:r - 7]
        cut = jnp.where(s1 == a[r:r + 1], cv, cut)
    e1 = jnp.exp(s1 - a[0:1]) / z
    e2 = jnp.exp(s2 - b[0:1])
    return cut, e1, e2


def _merge_kernel(x_ref, attn_ref, yp_ref, gates_ref, mod_ref, wout_ref, gpost_ref, gffn_ref, wq_ref, keys_ref,
                  xmid_ref, hf_ref, s2_ref, e2_ref, cut_ref, e1_ref):
    m = mod_ref[0]
    ga = gates_ref[0, :, 0:D_MODEL].astype(F32)
    gb = gates_ref[0, :, D_MODEL:2 * D_MODEL].astype(F32)
    merged = ga * attn_ref[0].astype(F32) + gb * yp_ref[0].astype(F32)
    y = jnp.dot(merged.astype(BF16), wout_ref[...], preferred_element_type=F32)
    x_mid = x_ref[0] + m[2:3] * (_rms(y) * gpost_ref[...])
    xmid_ref[0] = x_mid
    hf = (_rms(x_mid) * gffn_ref[...]) * (1.0 + m[4:5]) + m[3:4]
    hfb = hf.astype(BF16)
    hf_ref[0] = hfb
    q = jnp.dot(hfb, wq_ref[...], preferred_element_type=F32).astype(BF16)
    k1 = keys_ref[0]
    k2 = keys_ref[1]
    ts = q.shape[0]
    for hd in range(PEER_HEADS):
        base = hd * 2 * PEER_HALF
        s1 = lax.dot_general(k1, q[:, base:base + PEER_HALF], _NT, preferred_element_type=F32)
        s2 = lax.dot_general(k2, q[:, base + PEER_HALF:base + 2 * PEER_HALF], _NT, preferred_element_type=F32)
        for lt in range(ts // LANE):
            sl = slice(lt * LANE, (lt + 1) * LANE)
            cut, e1, e2 = _peer_stats(s1[:, sl], s2[:, sl])
            s2_ref[lt, hd] = s2[:, sl]
            e2_ref[lt, hd] = e2
            cut_ref[lt, :, hd, :] = cut
            e1_ref[lt, :, hd, :] = e1


def _merge_call(x, attn, ypool, gates, mod, w_out, g_post, g_ffn, wq, keys):
    b, s, _ = x.shape
    ts = min(TS_MERGE, s)
    nt = s // ts
    n_lt = b * s // LANE
    const2 = lambda i, j: (0, 0)
    tok3 = lambda i, j: (i, j, 0)
    lane_t = lambda i, j: (i * nt + j, 0, 0, 0)
    return pl.pallas_call(
        _merge_kernel,
        grid=(b, nt),
        in_specs=[pl.BlockSpec((1, ts, D_MODEL), tok3),
                  pl.BlockSpec((1, ts, D_MODEL), tok3),
                  pl.BlockSpec((1, ts, D_MODEL), tok3),
                  pl.BlockSpec((1, ts, 2 * D_MODEL), tok3),
                  pl.BlockSpec((1, 6, D_MODEL), lambda i, j: (i, 0, 0)),
                  pl.BlockSpec((D_MODEL, D_MODEL), const2),
                  pl.BlockSpec((1, D_MODEL), const2),
                  pl.BlockSpec((1, D_MODEL), const2),
                  pl.BlockSpec((D_MODEL, PEER_HEADS * 2 * PEER_HALF), const2),
                  pl.BlockSpec((2, N_KEYS, PEER_HALF), lambda i, j: (0, 0, 0))],
        out_specs=[pl.BlockSpec((1, ts, D_MODEL), tok3),
                   pl.BlockSpec((1, ts, D_MODEL), tok3),
                   pl.BlockSpec((ts // LANE, PEER_HEADS, N_KEYS, LANE), lane_t),
                   pl.BlockSpec((ts // LANE, PEER_HEADS, N_KEYS, LANE), lane_t),
                   pl.BlockSpec((ts // LANE, N_KEYS, PEER_HEADS, LANE), lane_t),
                   pl.BlockSpec((ts // LANE, N_KEYS, PEER_HEADS, LANE), lane_t)],
        out_shape=[jax.ShapeDtypeStruct((b, s, D_MODEL), F32),
                   jax.ShapeDtypeStruct((b, s, D_MODEL), BF16),
                   jax.ShapeDtypeStruct((n_lt, PEER_HEADS, N_KEYS, LANE), F32),
                   jax.ShapeDtypeStruct((n_lt, PEER_HEADS, N_KEYS, LANE), F32),
                   jax.ShapeDtypeStruct((n_lt, N_KEYS, PEER_HEADS, LANE), F32),
                   jax.ShapeDtypeStruct((n_lt, N_KEYS, PEER_HEADS, LANE), F32)],
        compiler_params=pltpu.CompilerParams(vmem_limit_bytes=VMEM_LIMIT),
        name="merge",
    )(x, attn, ypool, gates, mod, w_out, g_post, g_ffn, wq, keys)


def _peer_kernel(hf_ref, u_ref, vt_ref, s2_ref, e2_ref, cut_ref, e1_ref, xmid_ref, mod_ref, gpost_ref,
                 o_ref, acc_ref, act_ref, a_ref):
    e = pl.program_id(2)
    tt = hf_ref.shape[1]
    n_lt = tt // LANE

    @pl.when(e == 0)
    def _():
        acc_ref[...] = jnp.zeros_like(acc_ref)

    act_ref[...] = lax.dot_general(u_ref[...], hf_ref[0], _NT, preferred_element_type=F32)

    def body(it, carry):
        i = it // n_lt
        lt = it % n_lt
        lanes = pl.ds(pl.multiple_of(lt * LANE, LANE), LANE)
        cut_all = cut_ref[lt, i]
        e1_all = e1_ref[lt, i]
        for jc in range(2):
            w = jnp.zeros((64, LANE), F32)
            for hd in range(PEER_HEADS):
                s2 = s2_ref[lt, hd, jc * 64:(jc + 1) * 64, :]
                e2 = e2_ref[lt, hd, jc * 64:(jc + 1) * 64, :]
                w = w + jnp.where(s2 >= cut_all[hd:hd + 1], e2, 0.0) * e1_all[hd:hd + 1]
            rows = pl.ds(pl.multiple_of(i * N_KEYS + jc * 64, 64), 64)
            a = act_ref[rows, lanes]
            gelu = 0.5 * a * (1.0 + lax.erf(a * 0.7071067811865476))
            a_ref[rows, lanes] = (w * gelu).astype(BF16)
        return carry

    lax.fori_loop(0, EI_PEER * n_lt, body, 0)

    acc_ref[...] += jnp.dot(vt_ref[...], a_ref[...], preferred_element_type=F32)

    @pl.when(e == pl.num_programs(2) - 1)
    def _():
        m = mod_ref[0]
        y = acc_ref[...].T
        o_ref[0] = xmid_ref[0] + m[5:6] * (_rms(y) * gpost_ref[...])


def _peer_call(hf, u_b, vt_b, s2t, e2t, cutt, e1t, x_mid, mod, g_post):
    b, s, _ = hf.shape
    tt = min(TT_PEER, s)
    nt = s // tt
    et = EI_PEER * N_KEYS
    ne = N_KEYS // EI_PEER
    tok3 = lambda i, j, e: (i, j, 0)
    lane_t = lambda i, j, e: (i * nt + j, 0, 0, 0)
    return pl.pallas_call(
        _peer_kernel,
        grid=(b, nt, ne),
        in_specs=[pl.BlockSpec((1, tt, D_MODEL), tok3),
                  pl.BlockSpec((et, D_MODEL), lambda i, j, e: (e, 0)),
                  pl.BlockSpec((D_MODEL, et), lambda i, j, e: (0, e)),
                  pl.BlockSpec((tt // LANE, PEER_HEADS, N_KEYS, LANE), lane_t),
                  pl.BlockSpec((tt // LANE, PEER_HEADS, N_KEYS, LANE), lane_t),
                  pl.BlockSpec((tt // LANE, EI_PEER, PEER_HEADS, LANE), lambda i, j, e: (i * nt + j, e, 0, 0)),
                  pl.BlockSpec((tt // LANE, EI_PEER, PEER_HEADS, LANE), lambda i, j, e: (i * nt + j, e, 0, 0)),
                  pl.BlockSpec((1, tt, D_MODEL), tok3),
                  pl.BlockSpec((1, 6, D_MODEL), lambda i, j, e: (i, 0, 0)),
                  pl.BlockSpec((1, D_MODEL), lambda i, j, e: (0, 0))],
        out_specs=pl.BlockSpec((1, tt, D_MODEL), tok3),
        out_shape=jax.ShapeDtypeStruct((b, s, D_MODEL), F32),
        scratch_shapes=[pltpu.VMEM((D_MODEL, tt), F32),
                        pltpu.VMEM((et, tt), F32),
                        pltpu.VMEM((et, tt), BF16)],
        compiler_params=pltpu.CompilerParams(
            vmem_limit_bytes=VMEM_LIMIT,
            dimension_semantics=("arbitrary", "arbitrary", "arbitrary")),
        name="peer",
    )(hf, u_b, vt_b, s2t, e2t, cutt, e1t, x_mid, mod, g_post)


def _rope_tables(seq):
    rows = seq // GRID_W
    row = jnp.broadcast_to(jnp.arange(rows, dtype=F32)[:, None], (rows, GRID_W)).reshape(-1)
    col = jnp.broadcast_to(jnp.arange(GRID_W, dtype=F32)[None, :], (rows, GRID_W)).reshape(-1)
    n_freq = ROPE // 4
    inv_freq = ROPE_BASE ** (-jnp.arange(n_freq, dtype=F32) / n_freq)
    ar = row[:, None] * inv_freq
    ac = col[:, None] * inv_freq
    cos = jnp.concatenate([jnp.cos(ar), jnp.cos(ar), jnp.cos(ac), jnp.cos(ac)], axis=1)
    sin = jnp.concatenate([-jnp.sin(ar), jnp.sin(ar), -jnp.sin(ac), jnp.sin(ac)], axis=1)
    return cos, sin


def _swap_halves(w):
    q = ROPE // 4
    return jnp.concatenate([w[..., q:2 * q], w[..., 0:q], w[..., 3 * q:4 * q], w[..., 2 * q:3 * q]], axis=-1)


def kernel(x, c, ctx, c_ctx, w_ada, b_ada, g_mix_pre, g_mix_post, g_ffn_pre, g_ffn_post, w_in, g_q_lora, w_uq,
           g_kv_lora, w_ukv, w_pool, pool_scale, w_out, peer_wq, peer_keys, peer_u, peer_v):
    assert w_ada.shape[0] == 1, "single-layer block"
    b, s, _ = x.shape
    layer = 0

    w_in_l = w_in[layer]
    k_rope_w = w_in_l[:, KV_LORA:KV_IN]
    win_r = jnp.concatenate([w_in_l[:, 0:KV_IN], _swap_halves(k_rope_w), w_in_l[:, KV_IN:]], axis=1).astype(BF16)
    wuq_h = w_uq[layer].reshape(Q_LORA, N_HEADS, QK_HEAD)
    wuq_r = jnp.concatenate([wuq_h, _swap_halves(wuq_h[..., NOPE:])], axis=-1).reshape(Q_LORA, N_HEADS * QH_W)
    wuq_r = wuq_r.astype(BF16)
    wukv_h = w_ukv[layer].reshape(KV_LORA, N_HEADS, NOPE + VHEAD)
    wk_abs = jnp.transpose(wukv_h[..., 0:NOPE], (1, 2, 0)).astype(BF16)
    wv_up = jnp.transpose(wukv_h[..., NOPE:], (1, 0, 2)).astype(BF16)
    cos_t, sin_t = _rope_tables(s)
    u_b = peer_u[layer].astype(BF16)
    vt_b = peer_v[layer].T.astype(BF16)
    row = lambda v: v[layer][None, :]

    n_rows = -(-(b + 1) // 8) * 8
    cc = jnp.concatenate([c, c_ctx[None, :], jnp.zeros((n_rows - b - 1, D_MODEL), F32)], axis=0)
    mod_all = _mod_call(cc, w_ada[layer], b_ada[layer][None, :])
    mod = mod_all[:b].reshape(b, 6, D_MODEL)
    mod_c = mod_all[b].reshape(6, D_MODEL)

    k_ctx = _ctx_call(ctx, mod_c, row(g_mix_pre), win_r[:, 0:256], row(g_kv_lora))
    k_lat, q, pool_in, gates = _proj_call(x, mod, row(g_mix_pre), win_r, row(g_q_lora), wuq_r, row(g_kv_lora),
                                          wk_abs, cos_t, sin_t)
    k_all = jnp.concatenate([k_ctx, k_lat], axis=1)
    attn = _attn_call(q, k_all, wv_up)
    ypool = _pool_call(pool_in, w_pool[layer].astype(BF16), row(pool_scale))
    x_mid, hf, s2t, e2t, cutt, e1t = _merge_call(
        x, attn, ypool, gates, mod, w_out[layer].astype(BF16), row(g_mix_post), row(g_ffn_pre),
        peer_wq[layer].astype(BF16), peer_keys[layer].astype(BF16))
    return _peer_call(hf, u_b, vt_b, s2t, e2t, cutt, e1t, x_mid, mod, row(g_ffn_post))
```

```python
import functools

import jax
import jax.numpy as jnp
from jax import lax
from jax.experimental import pallas as pl
from jax.experimental.pallas import tpu as pltpu

F32 = jnp.float32
BF16 = jnp.bfloat16

D_MODEL = 1024
GRID_W = 64
N_HEADS = 8
NOPE = 128
ROPE = 64
VHEAD = 128
QK_HEAD = NOPE + ROPE
Q_LORA = 256
KV_LORA = 128
ROPE_BASE = 10000.0
ATTN_SCALE = QK_HEAD ** -0.5
POOL_GROUPS = 4
POOL_GROUP_DIM = D_MODEL // POOL_GROUPS
POOL_MAX_HALF = 1 << (POOL_GROUPS - 1)
KV_IN = KV_LORA + ROPE
PEER_HEADS = 8
N_KEYS = 128
PEER_HALF = 128
PEER_TOPK = 16
EPS = 1e-6

WIN_SMALL = 512
WIN_POOL = WIN_SMALL
WIN_GATE = WIN_POOL + D_MODEL
WIN_TOTAL = WIN_GATE + 2 * D_MODEL
QH_W = 256

LANE = 128
PEER_LT = 256
VMEM_LIMIT = 56 * 1024 * 1024

TS_PROJ = 512
TQ_ATTN = 256
TS_MERGE = 256
TT_PEER = 512
EI_PEER = 8

_NT = (((1,), (1,)), ((), ()))


def _rms(x):
    return x * lax.rsqrt(jnp.mean(x * x, axis=-1, keepdims=True) + EPS)


def _mod_kernel(cc_ref, w_ref, b_ref, o_ref):
    cc = cc_ref[...]
    a = cc * jax.nn.sigmoid(cc)
    o_ref[...] = jnp.dot(a, w_ref[...], preferred_element_type=F32) + b_ref[...]


def _mod_call(cc, w_ada, b_ada):
    rows = cc.shape[0]
    n = w_ada.shape[1]
    bn = 1024
    return pl.pallas_call(
        _mod_kernel,
        grid=(n // bn,),
        in_specs=[pl.BlockSpec((rows, D_MODEL), lambda j: (0, 0)),
                  pl.BlockSpec((D_MODEL, bn), lambda j: (0, j)),
                  pl.BlockSpec((1, bn), lambda j: (0, j))],
        out_specs=pl.BlockSpec((rows, bn), lambda j: (0, j)),
        out_shape=jax.ShapeDtypeStruct((rows, n), F32),
        compiler_params=pltpu.CompilerParams(vmem_limit_bytes=VMEM_LIMIT),
        name="mod",
    )(cc, w_ada, b_ada)


def _ctx_kernel(ctx_ref, mod_ref, gpre_ref, win_ref, gkv_ref, o_ref):
    x = ctx_ref[0]
    m = mod_ref[...]
    h = _rms(x) * gpre_ref[...]
    h = h * (1.0 + m[1:2]) + m[0:1]
    p = jnp.dot(h.astype(BF16), win_ref[...], preferred_element_type=F32)
    ckv_n = _rms(p[:, :KV_LORA]) * gkv_ref[...]
    o_ref[0, :, 0:KV_LORA] = ckv_n.astype(BF16)
    o_ref[0, :, KV_LORA:KV_IN] = p[:, KV_LORA:KV_IN].astype(BF16)


def _ctx_call(ctx, mod_c, g_pre, win_r, g_kv):
    b, lc, _ = ctx.shape
    return pl.pallas_call(
        _ctx_kernel,
        grid=(b,),
        in_specs=[pl.BlockSpec((1, lc, D_MODEL), lambda i: (i, 0, 0)),
                  pl.BlockSpec((6, D_MODEL), lambda i: (0, 0)),
                  pl.BlockSpec((1, D_MODEL), lambda i: (0, 0)),
                  pl.BlockSpec((D_MODEL, 256), lambda i: (0, 0)),
                  pl.BlockSpec((1, KV_LORA), lambda i: (0, 0))],
        out_specs=pl.BlockSpec((1, lc, KV_IN), lambda i: (i, 0, 0)),
        out_shape=jax.ShapeDtypeStruct((b, lc, KV_IN), BF16),
        compiler_params=pltpu.CompilerParams(vmem_limit_bytes=VMEM_LIMIT),
        name="ctx_kv",
    )(ctx, mod_c, g_pre, win_r, g_kv)


def _proj_kernel(x_ref, mod_ref, gpre_ref, win_ref, gq_ref, wuq_ref, gkv_ref, wk_ref, cos_ref, sin_ref,
                 kcat_ref, q_ref, pool_ref, gates_ref):
    x = x_ref[0]
    m = mod_ref[0]
    h = _rms(x) * gpre_ref[...]
    hb = (h * (1.0 + m[1:2]) + m[0:1]).astype(BF16)
    cos = cos_ref[...]
    sin = sin_ref[...]

    p0 = jnp.dot(hb, win_ref[:, 0:WIN_SMALL], preferred_element_type=F32)
    ckv_n = _rms(p0[:, 0:KV_LORA]) * gkv_ref[...]
    k_rot = p0[:, 128:192] * cos + p0[:, 192:256] * sin
    kcat_ref[0, :, 0:KV_LORA] = ckv_n.astype(BF16)
    kcat_ref[0, :, KV_LORA:KV_IN] = k_rot.astype(BF16)

    cqn = (_rms(p0[:, 256:512]) * gq_ref[...]).astype(BF16)
    for hd in range(N_HEADS):
        qh = jnp.dot(cqn, wuq_ref[:, hd * QH_W:(hd + 1) * QH_W], preferred_element_type=F32)
        q_abs = jnp.dot(qh[:, 0:NOPE].astype(BF16), wk_ref[hd], preferred_element_type=F32)
        q_rot = qh[:, 128:192] * cos + qh[:, 192:256] * sin
        q_ref[0, hd, :, 0:KV_LORA] = (q_abs * ATTN_SCALE).astype(BF16)
        q_ref[0, hd, :, KV_LORA:KV_IN] = (q_rot * ATTN_SCALE).astype(BF16)

    pool_ref[0] = jnp.dot(hb, win_ref[:, WIN_POOL:WIN_GATE], preferred_element_type=F32).astype(BF16)
    for half in range(2):
        lo = WIN_GATE + half * D_MODEL
        g = jnp.dot(hb, win_ref[:, lo:lo + D_MODEL], preferred_element_type=F32)
        gates_ref[0, :, half * D_MODEL:(half + 1) * D_MODEL] = jax.nn.sigmoid(g).astype(BF16)


def _proj_call(x, mod, g_pre, win_r, g_q, wuq_r, g_kv, wk_abs, cos_t, sin_t):
    b, s, _ = x.shape
    ts = min(TS_PROJ, s)
    const2 = lambda i, j: (0, 0)
    return pl.pallas_call(
        _proj_kernel,
        grid=(b, s // ts),
        in_specs=[pl.BlockSpec((1, ts, D_MODEL), lambda i, j: (i, j, 0)),
                  pl.BlockSpec((1, 6, D_MODEL), lambda i, j: (i, 0, 0)),
                  pl.BlockSpec((1, D_MODEL), const2),
                  pl.BlockSpec((D_MODEL, WIN_TOTAL), const2),
                  pl.BlockSpec((1, Q_LORA), const2),
                  pl.BlockSpec((Q_LORA, N_HEADS * QH_W), const2),
                  pl.BlockSpec((1, KV_LORA), const2),
                  pl.BlockSpec((N_HEADS, NOPE, KV_LORA), lambda i, j: (0, 0, 0)),
                  pl.BlockSpec((ts, ROPE), lambda i, j: (j, 0)),
                  pl.BlockSpec((ts, ROPE), lambda i, j: (j, 0))],
        out_specs=[pl.BlockSpec((1, ts, KV_IN), lambda i, j: (i, j, 0)),
                   pl.BlockSpec((1, N_HEADS, ts, KV_IN), lambda i, j: (i, 0, j, 0)),
                   pl.BlockSpec((1, ts, D_MODEL), lambda i, j: (i, j, 0)),
                   pl.BlockSpec((1, ts, 2 * D_MODEL), lambda i, j: (i, j, 0))],
        out_shape=[jax.ShapeDtypeStruct((b, s, KV_IN), BF16),
                   jax.ShapeDtypeStruct((b, N_HEADS, s, KV_IN), BF16),
                   jax.ShapeDtypeStruct((b, s, D_MODEL), BF16),
                   jax.ShapeDtypeStruct((b, s, 2 * D_MODEL), BF16)],
        compiler_params=pltpu.CompilerParams(vmem_limit_bytes=VMEM_LIMIT),
        name="proj",
    )(x, mod, g_pre, win_r, g_q, wuq_r, g_kv, wk_abs, cos_t, sin_t)


def _attn_kernel(q_ref, k_ref, wv_ref, o_ref):
    k = k_ref[0]
    v = k[:, 0:KV_LORA]
    for hd in range(N_HEADS):
        s = lax.dot_general(q_ref[0, hd], k, _NT, preferred_element_type=F32)
        p = jnp.exp(s - jnp.max(s, axis=-1, keepdims=True))
        l = jnp.sum(p, axis=-1, keepdims=True)
        o = jnp.dot(p.astype(BF16), v, preferred_element_type=F32) / l
        a = jnp.dot(o.astype(BF16), wv_ref[hd], preferred_element_type=F32)
        o_ref[0, :, hd * VHEAD:(hd + 1) * VHEAD] = a.astype(BF16)


def _attn_call(q, kall, wv):
    b, _, s, _ = q.shape
    lk = kall.shape[1]
    tq = min(TQ_ATTN, s)
    return pl.pallas_call(
        _attn_kernel,
        grid=(b, s // tq),
        in_specs=[pl.BlockSpec((1, N_HEADS, tq, KV_IN), lambda i, j: (i, 0, j, 0)),
                  pl.BlockSpec((1, lk, KV_IN), lambda i, j: (i, 0, 0)),
                  pl.BlockSpec((N_HEADS, KV_LORA, VHEAD), lambda i, j: (0, 0, 0))],
        out_specs=pl.BlockSpec((1, tq, D_MODEL), lambda i, j: (i, j, 0)),
        out_shape=jax.ShapeDtypeStruct((b, s, D_MODEL), BF16),
        compiler_params=pltpu.CompilerParams(vmem_limit_bytes=VMEM_LIMIT),
        name="attn",
    )(q, kall, wv)


def _pool_kernel(xp_ref, w_ref, ps_ref, o_ref):
    g = pl.program_id(1)
    x = xp_ref[0].astype(F32)
    n = x.shape[0]
    t = lax.broadcasted_iota(jnp.int32, x.shape, 0)
    pad = POOL_MAX_HALF
    zeros = jnp.zeros((pad, x.shape[1]), F32)
    xe = jnp.concatenate([zeros, x, zeros], axis=0)
    ne = n + 2 * pad

    def down(a, k):
        return pltpu.roll(a, k, 0)

    def up(a, k):
        return pltpu.roll(a, ne - k, 0)

    w = down(xe, 1) + xe
    for lvl in range(1, POOL_GROUPS):
        k = 1 << (lvl - 1)
        w = jnp.where(g >= lvl, down(w, k) + up(w, k), w)
    w = w[pad:pad + n]
    half = lax.shift_left(jnp.int32(1), g)
    count = (jnp.minimum(t + half, n) - jnp.maximum(t - half, 0)).astype(F32)
    pooled = (w / count - x).astype(BF16)
    y = jnp.dot(pooled, w_ref[0], preferred_element_type=F32) * ps_ref[...]
    o_ref[0] = y.astype(BF16)


def _pool_call(pool_in, w_pool, pool_scale):
    b, s, _ = pool_in.shape
    gd = POOL_GROUP_DIM
    return pl.pallas_call(
        _pool_kernel,
        grid=(b, POOL_GROUPS),
        in_specs=[pl.BlockSpec((1, s, gd), lambda i, g: (i, 0, g)),
                  pl.BlockSpec((1, gd, gd), lambda i, g: (g, 0, 0)),
                  pl.BlockSpec((1, gd), lambda i, g: (0, g))],
        out_specs=pl.BlockSpec((1, s, gd), lambda i, g: (i, 0, g)),
        out_shape=jax.ShapeDtypeStruct((b, s, D_MODEL), BF16),
        compiler_params=pltpu.CompilerParams(vmem_limit_bytes=VMEM_LIMIT),
        name="pool",
    )(pool_in, w_pool, pool_scale)


def _top16(s, rows16):
    out = jnp.full((PEER_TOPK, s.shape[1]), -jnp.inf, F32)
    rank = jnp.full(s.shape, float(PEER_TOPK), F32)
    for k in range(PEER_TOPK):
        m = jnp.max(s, axis=0, keepdims=True)
        out = jnp.where(rows16 == k, m, out)
        hit = s == m
        rank = jnp.where(hit, float(k), rank)
        s = jnp.where(hit, -jnp.inf, s)
    return out, rank


def _peer_stats(s1, s2):
    lanes = s1.shape[1]
    rows16 = lax.broadcasted_iota(jnp.int32, (PEER_TOPK, lanes), 0)
    rows8 = lax.broadcasted_iota(jnp.int32, (8, lanes), 0)
    a, _ = _top16(s1, rows16)
    b, rank2 = _top16(s2, rows16)
    pieces = [a + b[0:1]]
    pieces += [a[0:8] + b[c:c + 1] for c in range(1, 8)]
    pieces += [b[8:16] + a[0:1]]
    cand = jnp.concatenate(pieces, axis=0)
    work = cand
    thr = None
    for _ in range(PEER_TOPK):
        thr = jnp.max(work, axis=0, keepdims=True)
        work = jnp.where(work == thr, -jnp.inf, work)
    top = cand[0:1]
    sel = cand >= thr
    z = jnp.sum(jnp.where(sel, jnp.exp(cand - top), 0.0), axis=0, keepdims=True)
    self = sel.astype(F32)
    n_hi = self[8:16]
    n_lo = self[0:8]
    for c in range(1, 8):
        n_lo = n_lo + self[8 + 8 * c:16 + 8 * c]
    extra = jnp.sum(self[72:80], axis=0, keepdims=True)
    n_lo = n_lo + jnp.where(rows8 == 0, extra, 0.0)
    cnt = jnp.zeros(s1.shape, F32)
    for r in range(PEER_TOPK):
        n_r = n_lo[r:r + 1] if r < 8 else n_hi[r - 8:r - 7]
        cnt = jnp.where(s1 == a[r:r + 1], n_r, cnt)
    e1 = jnp.exp(s1 - a[0:1]) / z
    e2 = jnp.exp(s2 - b[0:1])
    return cnt, e1, rank2, e2


def _merge_kernel(x_ref, attn_ref, yp_ref, gates_ref, mod_ref, wout_ref, gpost_ref, gffn_ref, wq_ref, keys_ref,
                  xmid_ref, hf_ref, rk_ref, e2_ref, cnt_ref, e1_ref):
    m = mod_ref[0]
    ga = gates_ref[0, :, 0:D_MODEL].astype(F32)
    gb = gates_ref[0, :, D_MODEL:2 * D_MODEL].astype(F32)
    merged = ga * attn_ref[0].astype(F32) + gb * yp_ref[0].astype(F32)
    y = jnp.dot(merged.astype(BF16), wout_ref[...], preferred_element_type=F32)
    x_mid = x_ref[0] + m[2:3] * (_rms(y) * gpost_ref[...])
    xmid_ref[0] = x_mid
    hf = (_rms(x_mid) * gffn_ref[...]) * (1.0 + m[4:5]) + m[3:4]
    hfb = hf.astype(BF16)
    hf_ref[0] = hfb
    q = jnp.dot(hfb, wq_ref[...], preferred_element_type=F32).astype(BF16)
    k1 = keys_ref[0]
    k2 = keys_ref[1]
    ts = q.shape[0]
    for hd in range(PEER_HEADS):
        base = hd * 2 * PEER_HALF
        s1 = lax.dot_general(k1, q[:, base:base + PEER_HALF], _NT, preferred_element_type=F32)
        s2 = lax.dot_general(k2, q[:, base + PEER_HALF:base + 2 * PEER_HALF], _NT, preferred_element_type=F32)
        for lt in range(ts // LANE):
            sl = slice(lt * LANE, (lt + 1) * LANE)
            blk, off = divmod(lt * LANE, PEER_LT)
            dst = slice(off, off + LANE)
            cnt, e1, rank2, e2 = _peer_stats(s1[:, sl], s2[:, sl])
            rk_ref[blk, hd, :, dst] = rank2.astype(BF16)
            e2_ref[blk, hd, :, dst] = e2.astype(BF16)
            cnt_ref[blk, :, hd, dst] = cnt.astype(BF16)
            e1_ref[blk, :, hd, dst] = e1.astype(BF16)


def _merge_call(x, attn, ypool, gates, mod, w_out, g_post, g_ffn, wq, keys):
    b, s, _ = x.shape
    ts = min(TS_MERGE, s)
    nt = s // ts
    n_lt = b * s // PEER_LT
    const2 = lambda i, j: (0, 0)
    tok3 = lambda i, j: (i, j, 0)
    lane_t = lambda i, j: (i * nt + j, 0, 0, 0)
    return pl.pallas_call(
        _merge_kernel,
        grid=(b, nt),
        in_specs=[pl.BlockSpec((1, ts, D_MODEL), tok3),
                  pl.BlockSpec((1, ts, D_MODEL), tok3),
                  pl.BlockSpec((1, ts, D_MODEL), tok3),
                  pl.BlockSpec((1, ts, 2 * D_MODEL), tok3),
                  pl.BlockSpec((1, 6, D_MODEL), lambda i, j: (i, 0, 0)),
                  pl.BlockSpec((D_MODEL, D_MODEL), const2),
                  pl.BlockSpec((1, D_MODEL), const2),
                  pl.BlockSpec((1, D_MODEL), const2),
                  pl.BlockSpec((D_MODEL, PEER_HEADS * 2 * PEER_HALF), const2),
                  pl.BlockSpec((2, N_KEYS, PEER_HALF), lambda i, j: (0, 0, 0))],
        out_specs=[pl.BlockSpec((1, ts, D_MODEL), tok3),
                   pl.BlockSpec((1, ts, D_MODEL), tok3),
                   pl.BlockSpec((ts // PEER_LT, PEER_HEADS, N_KEYS, PEER_LT), lane_t),
                   pl.BlockSpec((ts // PEER_LT, PEER_HEADS, N_KEYS, PEER_LT), lane_t),
                   pl.BlockSpec((ts // PEER_LT, N_KEYS, PEER_HEADS, PEER_LT), lane_t),
                   pl.BlockSpec((ts // PEER_LT, N_KEYS, PEER_HEADS, PEER_LT), lane_t)],
        out_shape=[jax.ShapeDtypeStruct((b, s, D_MODEL), F32),
                   jax.ShapeDtypeStruct((b, s, D_MODEL), BF16),
                   jax.ShapeDtypeStruct((n_lt, PEER_HEADS, N_KEYS, PEER_LT), BF16),
                   jax.ShapeDtypeStruct((n_lt, PEER_HEADS, N_KEYS, PEER_LT), BF16),
                   jax.ShapeDtypeStruct((n_lt, N_KEYS, PEER_HEADS, PEER_LT), BF16),
                   jax.ShapeDtypeStruct((n_lt, N_KEYS, PEER_HEADS, PEER_LT), BF16)],
        compiler_params=pltpu.CompilerParams(vmem_limit_bytes=VMEM_LIMIT),
        name="merge",
    )(x, attn, ypool, gates, mod, w_out, g_post, g_ffn, wq, keys)


def _peer_kernel(hf_ref, u_ref, vt_ref, rk_ref, e2_ref, cnt_ref, e1_ref, xmid_ref, mod_ref, gpost_ref,
                 o_ref, acc_ref, act_ref, a_ref):
    e = pl.program_id(2)
    tt = hf_ref.shape[1]
    n_lt = tt // PEER_LT

    @pl.when(e == 0)
    def _():
        acc_ref[...] = jnp.zeros_like(acc_ref)

    act_ref[...] = lax.dot_general(u_ref[...], hf_ref[0], _NT, preferred_element_type=F32)

    def body(it, carry):
        i = it // n_lt
        lt = it % n_lt
        lanes = pl.ds(pl.multiple_of(lt * PEER_LT, PEER_LT), PEER_LT)
        cnt_all = cnt_ref[lt, i]
        e1_all = e1_ref[lt, i]
        w = jnp.zeros((N_KEYS, PEER_LT), BF16)
        for hd in range(PEER_HEADS):
            e2 = e2_ref[lt, hd]
            hit = rk_ref[lt, hd] < cnt_all[hd:hd + 1]
            w = w + jnp.where(hit, e2, jnp.zeros_like(e2)) * e1_all[hd:hd + 1]
        rows = pl.ds(pl.multiple_of(i * N_KEYS, N_KEYS), N_KEYS)
        a = act_ref[rows, lanes]
        gelu = 0.5 * a * (1.0 + lax.erf(a * 0.7071067811865476))
        a_ref[rows, lanes] = w * gelu.astype(BF16)
        return carry

    lax.fori_loop(0, EI_PEER * n_lt, body, 0)

    acc_ref[...] += jnp.dot(vt_ref[...], a_ref[...], preferred_element_type=F32)

    @pl.when(e == pl.num_programs(2) - 1)
    def _():
        m = mod_ref[0]
        y = acc_ref[...].T
        o_ref[0] = xmid_ref[0] + m[5:6] * (_rms(y) * gpost_ref[...])


def _peer_call(hf, u_b, vt_b, s2t, e2t, cutt, e1t, x_mid, mod, g_post):
    b, s, _ = hf.shape
    tt = min(TT_PEER, s)
    nt = s // tt
    et = EI_PEER * N_KEYS
    ne = N_KEYS // EI_PEER
    tok3 = lambda i, j, e: (i, j, 0)
    lane_t = lambda i, j, e: (i * nt + j, 0, 0, 0)
    return pl.pallas_call(
        _peer_kernel,
        grid=(b, nt, ne),
        in_specs=[pl.BlockSpec((1, tt, D_MODEL), tok3),
                  pl.BlockSpec((et, D_MODEL), lambda i, j, e: (e, 0)),
                  pl.BlockSpec((D_MODEL, et), lambda i, j, e: (0, e)),
                  pl.BlockSpec((tt // PEER_LT, PEER_HEADS, N_KEYS, PEER_LT), lane_t),
                  pl.BlockSpec((tt // PEER_LT, PEER_HEADS, N_KEYS, PEER_LT), lane_t),
                  pl.BlockSpec((tt // PEER_LT, EI_PEER, PEER_HEADS, PEER_LT), lambda i, j, e: (i * nt + j, e, 0, 0)),
                  pl.BlockSpec((tt // PEER_LT, EI_PEER, PEER_HEADS, PEER_LT), lambda i, j, e: (i * nt + j, e, 0, 0)),
                  pl.BlockSpec((1, tt, D_MODEL), tok3),
                  pl.BlockSpec((1, 6, D_MODEL), lambda i, j, e: (i, 0, 0)),
                  pl.BlockSpec((1, D_MODEL), lambda i, j, e: (0, 0))],
        out_specs=pl.BlockSpec((1, tt, D_MODEL), tok3),
        out_shape=jax.ShapeDtypeStruct((b, s, D_MODEL), F32),
        scratch_shapes=[pltpu.VMEM((D_MODEL, tt), F32),
                        pltpu.VMEM((et, tt), F32),
                        pltpu.VMEM((et, tt), BF16)],
        compiler_params=pltpu.CompilerParams(
            vmem_limit_bytes=VMEM_LIMIT,
            dimension_semantics=("arbitrary", "arbitrary", "arbitrary")),
        name="peer",
    )(hf, u_b, vt_b, s2t, e2t, cutt, e1t, x_mid, mod, g_post)


def _rope_tables(seq):
    rows = seq // GRID_W
    row = jnp.broadcast_to(jnp.arange(rows, dtype=F32)[:, None], (rows, GRID_W)).reshape(-1)
    col = jnp.broadcast_to(jnp.arange(GRID_W, dtype=F32)[None, :], (rows, GRID_W)).reshape(-1)
    n_freq = ROPE // 4
    inv_freq = ROPE_BASE ** (-jnp.arange(n_freq, dtype=F32) / n_freq)
    ar = row[:, None] * inv_freq
    ac = col[:, None] * inv_freq
    cos = jnp.concatenate([jnp.cos(ar), jnp.cos(ar), jnp.cos(ac), jnp.cos(ac)], axis=1)
    sin = jnp.concatenate([-jnp.sin(ar), jnp.sin(ar), -jnp.sin(ac), jnp.sin(ac)], axis=1)
    return cos, sin


def _swap_halves(w):
    q = ROPE // 4
    return jnp.concatenate([w[..., q:2 * q], w[..., 0:q], w[..., 3 * q:4 * q], w[..., 2 * q:3 * q]], axis=-1)


def kernel(x, c, ctx, c_ctx, w_ada, b_ada, g_mix_pre, g_mix_post, g_ffn_pre, g_ffn_post, w_in, g_q_lora, w_uq,
           g_kv_lora, w_ukv, w_pool, pool_scale, w_out, peer_wq, peer_keys, peer_u, peer_v):
    assert w_ada.shape[0] == 1, "single-layer block"
    b, s, _ = x.shape
    layer = 0

    w_in_l = w_in[layer]
    k_rope_w = w_in_l[:, KV_LORA:KV_IN]
    win_r = jnp.concatenate([w_in_l[:, 0:KV_IN], _swap_halves(k_rope_w), w_in_l[:, KV_IN:]], axis=1).astype(BF16)
    wuq_h = w_uq[layer].reshape(Q_LORA, N_HEADS, QK_HEAD)
    wuq_r = jnp.concatenate([wuq_h, _swap_halves(wuq_h[..., NOPE:])], axis=-1).reshape(Q_LORA, N_HEADS * QH_W)
    wuq_r = wuq_r.astype(BF16)
    wukv_h = w_ukv[layer].reshape(KV_LORA, N_HEADS, NOPE + VHEAD)
    wk_abs = jnp.transpose(wukv_h[..., 0:NOPE], (1, 2, 0)).astype(BF16)
    wv_up = jnp.transpose(wukv_h[..., NOPE:], (1, 0, 2)).astype(BF16)
    cos_t, sin_t = _rope_tables(s)
    u_b = peer_u[layer].astype(BF16)
    vt_b = peer_v[layer].T.astype(BF16)
    row = lambda v: v[layer][None, :]

    n_rows = -(-(b + 1) // 8) * 8
    cc = jnp.concatenate([c, c_ctx[None, :], jnp.zeros((n_rows - b - 1, D_MODEL), F32)], axis=0)
    mod_all = _mod_call(cc, w_ada[layer], b_ada[layer][None, :])
    mod = mod_all[:b].reshape(b, 6, D_MODEL)
    mod_c = mod_all[b].reshape(6, D_MODEL)

    k_ctx = _ctx_call(ctx, mod_c, row(g_mix_pre), win_r[:, 0:256], row(g_kv_lora))
    k_lat, q, pool_in, gates = _proj_call(x, mod, row(g_mix_pre), win_r, row(g_q_lora), wuq_r, row(g_kv_lora),
                                          wk_abs, cos_t, sin_t)
    k_all = jnp.concatenate([k_ctx, k_lat], axis=1)
    attn = _attn_call(q, k_all, wv_up)
    ypool = _pool_call(pool_in, w_pool[layer].astype(BF16), row(pool_scale))
    x_mid, hf, s2t, e2t, cutt, e1t = _merge_call(
        x, attn, ypool, gates, mod, w_out[layer].astype(BF16), row(g_mix_post), row(g_ffn_pre),
        peer_wq[layer].astype(BF16), peer_keys[layer].astype(BF16))
    return _peer_call(hf, u_b, vt_b, s2t, e2t, cutt, e1t, x_mid, mod, row(g_ffn_post))
```

```python
import functools

import jax
import jax.numpy as jnp
from jax import lax
from jax.experimental import pallas as pl
from jax.experimental.pallas import tpu as pltpu

F32 = jnp.float32
BF16 = jnp.bfloat16

D_MODEL = 1024
GRID_W = 64
N_HEADS = 8
NOPE = 128
ROPE = 64
VHEAD = 128
QK_HEAD = NOPE + ROPE
Q_LORA = 256
KV_LORA = 128
ROPE_BASE = 10000.0
ATTN_SCALE = QK_HEAD ** -0.5
POOL_GROUPS = 4
POOL_GROUP_DIM = D_MODEL // POOL_GROUPS
POOL_MAX_HALF = 1 << (POOL_GROUPS - 1)
KV_IN = KV_LORA + ROPE
PEER_HEADS = 8
N_KEYS = 128
PEER_HALF = 128
PEER_TOPK = 16
EPS = 1e-6

WIN_SMALL = 512
WIN_POOL = WIN_SMALL
WIN_GATE = WIN_POOL + D_MODEL
WIN_TOTAL = WIN_GATE + 2 * D_MODEL
QH_W = 256

LANE = 128
PEER_LT = 256
VMEM_LIMIT = 56 * 1024 * 1024

TS_PROJ = 512
TQ_ATTN = 256
TS_MERGE = 256
TT_PEER = 1024
EI_PEER = 8

_NT = (((1,), (1,)), ((), ()))


def _rms(x):
    return x * lax.rsqrt(jnp.mean(x * x, axis=-1, keepdims=True) + EPS)


def _mod_kernel(cc_ref, w_ref, b_ref, o_ref):
    cc = cc_ref[...]
    a = cc * jax.nn.sigmoid(cc)
    o_ref[...] = jnp.dot(a, w_ref[...], preferred_element_type=F32) + b_ref[...]


def _mod_call(cc, w_ada, b_ada):
    rows = cc.shape[0]
    n = w_ada.shape[1]
    bn = 1024
    return pl.pallas_call(
        _mod_kernel,
        grid=(n // bn,),
        in_specs=[pl.BlockSpec((rows, D_MODEL), lambda j: (0, 0)),
                  pl.BlockSpec((D_MODEL, bn), lambda j: (0, j)),
                  pl.BlockSpec((1, bn), lambda j: (0, j))],
        out_specs=pl.BlockSpec((rows, bn), lambda j: (0, j)),
        out_shape=jax.ShapeDtypeStruct((rows, n), F32),
        compiler_params=pltpu.CompilerParams(vmem_limit_bytes=VMEM_LIMIT),
        name="mod",
    )(cc, w_ada, b_ada)


def _ctx_kernel(ctx_ref, mod_ref, gpre_ref, win_ref, gkv_ref, o_ref):
    x = ctx_ref[0]
    m = mod_ref[...]
    h = _rms(x) * gpre_ref[...]
    h = h * (1.0 + m[1:2]) + m[0:1]
    p = jnp.dot(h.astype(BF16), win_ref[...], preferred_element_type=F32)
    ckv_n = _rms(p[:, :KV_LORA]) * gkv_ref[...]
    o_ref[0, :, 0:KV_LORA] = ckv_n.astype(BF16)
    o_ref[0, :, KV_LORA:KV_IN] = p[:, KV_LORA:KV_IN].astype(BF16)


def _ctx_call(ctx, mod_c, g_pre, win_r, g_kv):
    b, lc, _ = ctx.shape
    return pl.pallas_call(
        _ctx_kernel,
        grid=(b,),
        in_specs=[pl.BlockSpec((1, lc, D_MODEL), lambda i: (i, 0, 0)),
                  pl.BlockSpec((6, D_MODEL), lambda i: (0, 0)),
                  pl.BlockSpec((1, D_MODEL), lambda i: (0, 0)),
                  pl.BlockSpec((D_MODEL, 256), lambda i: (0, 0)),
                  pl.BlockSpec((1, KV_LORA), lambda i: (0, 0))],
        out_specs=pl.BlockSpec((1, lc, KV_IN), lambda i: (i, 0, 0)),
        out_shape=jax.ShapeDtypeStruct((b, lc, KV_IN), BF16),
        compiler_params=pltpu.CompilerParams(vmem_limit_bytes=VMEM_LIMIT),
        name="ctx_kv",
    )(ctx, mod_c, g_pre, win_r, g_kv)


def _proj_kernel(x_ref, mod_ref, gpre_ref, win_ref, gq_ref, wuq_ref, gkv_ref, wk_ref, cos_ref, sin_ref,
                 kcat_ref, q_ref, pool_ref, gates_ref):
    x = x_ref[0]
    m = mod_ref[0]
    h = _rms(x) * gpre_ref[...]
    hb = (h * (1.0 + m[1:2]) + m[0:1]).astype(BF16)
    cos = cos_ref[...]
    sin = sin_ref[...]

    p0 = jnp.dot(hb, win_ref[:, 0:WIN_SMALL], preferred_element_type=F32)
    ckv_n = _rms(p0[:, 0:KV_LORA]) * gkv_ref[...]
    k_rot = p0[:, 128:192] * cos + p0[:, 192:256] * sin
    kcat_ref[0, :, 0:KV_LORA] = ckv_n.astype(BF16)
    kcat_ref[0, :, KV_LORA:KV_IN] = k_rot.astype(BF16)

    cqn = (_rms(p0[:, 256:512]) * gq_ref[...]).astype(BF16)
    for hd in range(N_HEADS):
        qh = jnp.dot(cqn, wuq_ref[:, hd * QH_W:(hd + 1) * QH_W], preferred_element_type=F32)
        q_abs = jnp.dot(qh[:, 0:NOPE].astype(BF16), wk_ref[hd], preferred_element_type=F32)
        q_rot = qh[:, 128:192] * cos + qh[:, 192:256] * sin
        q_ref[0, hd, :, 0:KV_LORA] = (q_abs * ATTN_SCALE).astype(BF16)
        q_ref[0, hd, :, KV_LORA:KV_IN] = (q_rot * ATTN_SCALE).astype(BF16)

    pool_ref[0] = jnp.dot(hb, win_ref[:, WIN_POOL:WIN_GATE], preferred_element_type=F32).astype(BF16)
    for half in range(2):
        lo = WIN_GATE + half * D_MODEL
        g = jnp.dot(hb, win_ref[:, lo:lo + D_MODEL], preferred_element_type=F32)
        gates_ref[0, :, half * D_MODEL:(half + 1) * D_MODEL] = jax.nn.sigmoid(g).astype(BF16)


def _proj_call(x, mod, g_pre, win_r, g_q, wuq_r, g_kv, wk_abs, cos_t, sin_t):
    b, s, _ = x.shape
    ts = min(TS_PROJ, s)
    const2 = lambda i, j: (0, 0)
    return pl.pallas_call(
        _proj_kernel,
        grid=(b, s // ts),
        in_specs=[pl.BlockSpec((1, ts, D_MODEL), lambda i, j: (i, j, 0)),
                  pl.BlockSpec((1, 6, D_MODEL), lambda i, j: (i, 0, 0)),
                  pl.BlockSpec((1, D_MODEL), const2),
                  pl.BlockSpec((D_MODEL, WIN_TOTAL), const2),
                  pl.BlockSpec((1, Q_LORA), const2),
                  pl.BlockSpec((Q_LORA, N_HEADS * QH_W), const2),
                  pl.BlockSpec((1, KV_LORA), const2),
                  pl.BlockSpec((N_HEADS, NOPE, KV_LORA), lambda i, j: (0, 0, 0)),
                  pl.BlockSpec((ts, ROPE), lambda i, j: (j, 0)),
                  pl.BlockSpec((ts, ROPE), lambda i, j: (j, 0))],
        out_specs=[pl.BlockSpec((1, ts, KV_IN), lambda i, j: (i, j, 0)),
                   pl.BlockSpec((1, N_HEADS, ts, KV_IN), lambda i, j: (i, 0, j, 0)),
                   pl.BlockSpec((1, ts, D_MODEL), lambda i, j: (i, j, 0)),
                   pl.BlockSpec((1, ts, 2 * D_MODEL), lambda i, j: (i, j, 0))],
        out_shape=[jax.ShapeDtypeStruct((b, s, KV_IN), BF16),
                   jax.ShapeDtypeStruct((b, N_HEADS, s, KV_IN), BF16),
                   jax.ShapeDtypeStruct((b, s, D_MODEL), BF16),
                   jax.ShapeDtypeStruct((b, s, 2 * D_MODEL), BF16)],
        compiler_params=pltpu.CompilerParams(vmem_limit_bytes=VMEM_LIMIT),
        name="proj",
    )(x, mod, g_pre, win_r, g_q, wuq_r, g_kv, wk_abs, cos_t, sin_t)


def _attn_kernel(q_ref, k_ref, wv_ref, o_ref):
    k = k_ref[0]
    v = k[:, 0:KV_LORA]
    for hd in range(N_HEADS):
        s = lax.dot_general(q_ref[0, hd], k, _NT, preferred_element_type=F32)
        p = jnp.exp(s - jnp.max(s, axis=-1, keepdims=True))
        l = jnp.sum(p, axis=-1, keepdims=True)
        o = jnp.dot(p.astype(BF16), v, preferred_element_type=F32) / l
        a = jnp.dot(o.astype(BF16), wv_ref[hd], preferred_element_type=F32)
        o_ref[0, :, hd * VHEAD:(hd + 1) * VHEAD] = a.astype(BF16)


def _attn_call(q, kall, wv):
    b, _, s, _ = q.shape
    lk = kall.shape[1]
    tq = min(TQ_ATTN, s)
    return pl.pallas_call(
        _attn_kernel,
        grid=(b, s // tq),
        in_specs=[pl.BlockSpec((1, N_HEADS, tq, KV_IN), lambda i, j: (i, 0, j, 0)),
                  pl.BlockSpec((1, lk, KV_IN), lambda i, j: (i, 0, 0)),
                  pl.BlockSpec((N_HEADS, KV_LORA, VHEAD), lambda i, j: (0, 0, 0))],
        out_specs=pl.BlockSpec((1, tq, D_MODEL), lambda i, j: (i, j, 0)),
        out_shape=jax.ShapeDtypeStruct((b, s, D_MODEL), BF16),
        compiler_params=pltpu.CompilerParams(vmem_limit_bytes=VMEM_LIMIT),
        name="attn",
    )(q, kall, wv)


def _pool_kernel(xp_ref, w_ref, ps_ref, o_ref):
    g = pl.program_id(1)
    x = xp_ref[0].astype(F32)
    n = x.shape[0]
    t = lax.broadcasted_iota(jnp.int32, x.shape, 0)
    pad = POOL_MAX_HALF
    zeros = jnp.zeros((pad, x.shape[1]), F32)
    xe = jnp.concatenate([zeros, x, zeros], axis=0)
    ne = n + 2 * pad

    def down(a, k):
        return pltpu.roll(a, k, 0)

    def up(a, k):
        return pltpu.roll(a, ne - k, 0)

    w = down(xe, 1) + xe
    for lvl in range(1, POOL_GROUPS):
        k = 1 << (lvl - 1)
        w = jnp.where(g >= lvl, down(w, k) + up(w, k), w)
    w = w[pad:pad + n]
    half = lax.shift_left(jnp.int32(1), g)
    count = (jnp.minimum(t + half, n) - jnp.maximum(t - half, 0)).astype(F32)
    pooled = (w / count - x).astype(BF16)
    y = jnp.dot(pooled, w_ref[0], preferred_element_type=F32) * ps_ref[...]
    o_ref[0] = y.astype(BF16)


def _pool_call(pool_in, w_pool, pool_scale):
    b, s, _ = pool_in.shape
    gd = POOL_GROUP_DIM
    return pl.pallas_call(
        _pool_kernel,
        grid=(b, POOL_GROUPS),
        in_specs=[pl.BlockSpec((1, s, gd), lambda i, g: (i, 0, g)),
                  pl.BlockSpec((1, gd, gd), lambda i, g: (g, 0, 0)),
                  pl.BlockSpec((1, gd), lambda i, g: (0, g))],
        out_specs=pl.BlockSpec((1, s, gd), lambda i, g: (i, 0, g)),
        out_shape=jax.ShapeDtypeStruct((b, s, D_MODEL), BF16),
        compiler_params=pltpu.CompilerParams(vmem_limit_bytes=VMEM_LIMIT),
        name="pool",
    )(pool_in, w_pool, pool_scale)


def _top16(s, rows16):
    out = jnp.full((PEER_TOPK, s.shape[1]), -jnp.inf, F32)
    rank = jnp.full(s.shape, float(PEER_TOPK), F32)
    for k in range(PEER_TOPK):
        m = jnp.max(s, axis=0, keepdims=True)
        out = jnp.where(rows16 == k, m, out)
        hit = s == m
        rank = jnp.where(hit, float(k), rank)
        s = jnp.where(hit, -jnp.inf, s)
    return out, rank


def _peer_stats(s1, s2):
    lanes = s1.shape[1]
    rows16 = lax.broadcasted_iota(jnp.int32, (PEER_TOPK, lanes), 0)
    rows8 = lax.broadcasted_iota(jnp.int32, (8, lanes), 0)
    a, _ = _top16(s1, rows16)
    b, rank2 = _top16(s2, rows16)
    pieces = [a + b[0:1]]
    pieces += [a[0:8] + b[c:c + 1] for c in range(1, 8)]
    pieces += [b[8:16] + a[0:1]]
    cand = jnp.concatenate(pieces, axis=0)
    work = cand
    thr = None
    for _ in range(PEER_TOPK):
        thr = jnp.max(work, axis=0, keepdims=True)
        work = jnp.where(work == thr, -jnp.inf, work)
    top = cand[0:1]
    sel = cand >= thr
    z = jnp.sum(jnp.where(sel, jnp.exp(cand - top), 0.0), axis=0, keepdims=True)
    self = sel.astype(F32)
    n_hi = self[8:16]
    n_lo = self[0:8]
    for c in range(1, 8):
        n_lo = n_lo + self[8 + 8 * c:16 + 8 * c]
    extra = jnp.sum(self[72:80], axis=0, keepdims=True)
    n_lo = n_lo + jnp.where(rows8 == 0, extra, 0.0)
    cnt = jnp.zeros(s1.shape, F32)
    for r in range(PEER_TOPK):
        n_r = n_lo[r:r + 1] if r < 8 else n_hi[r - 8:r - 7]
        cnt = jnp.where(s1 == a[r:r + 1], n_r, cnt)
    e1 = jnp.exp(s1 - a[0:1]) / z
    e2 = jnp.exp(s2 - b[0:1])
    return cnt, e1, rank2, e2


def _merge_kernel(x_ref, attn_ref, yp_ref, gates_ref, mod_ref, wout_ref, gpost_ref, gffn_ref, wq_ref, keys_ref,
                  xmid_ref, hf_ref, rk_ref, e2_ref, cnt_ref, e1_ref):
    m = mod_ref[0]
    ga = gates_ref[0, :, 0:D_MODEL].astype(F32)
    gb = gates_ref[0, :, D_MODEL:2 * D_MODEL].astype(F32)
    merged = ga * attn_ref[0].astype(F32) + gb * yp_ref[0].astype(F32)
    y = jnp.dot(merged.astype(BF16), wout_ref[...], preferred_element_type=F32)
    x_mid = x_ref[0] + m[2:3] * (_rms(y) * gpost_ref[...])
    xmid_ref[0] = x_mid
    hf = (_rms(x_mid) * gffn_ref[...]) * (1.0 + m[4:5]) + m[3:4]
    hfb = hf.astype(BF16)
    hf_ref[0] = hfb
    q = jnp.dot(hfb, wq_ref[...], preferred_element_type=F32).astype(BF16)
    k1 = keys_ref[0]
    k2 = keys_ref[1]
    ts = q.shape[0]
    for hd in range(PEER_HEADS):
        base = hd * 2 * PEER_HALF
        s1 = lax.dot_general(k1, q[:, base:base + PEER_HALF], _NT, preferred_element_type=F32)
        s2 = lax.dot_general(k2, q[:, base + PEER_HALF:base + 2 * PEER_HALF], _NT, preferred_element_type=F32)
        for lt in range(ts // LANE):
            sl = slice(lt * LANE, (lt + 1) * LANE)
            blk, off = divmod(lt * LANE, PEER_LT)
            dst = slice(off, off + LANE)
            cnt, e1, rank2, e2 = _peer_stats(s1[:, sl], s2[:, sl])
            rk_ref[blk, hd, :, dst] = rank2.astype(BF16)
            e2_ref[blk, hd, :, dst] = e2.astype(BF16)
            cnt_ref[blk, :, hd, dst] = cnt.astype(BF16)
            e1_ref[blk, :, hd, dst] = e1.astype(BF16)


def _merge_call(x, attn, ypool, gates, mod, w_out, g_post, g_ffn, wq, keys):
    b, s, _ = x.shape
    ts = min(TS_MERGE, s)
    nt = s // ts
    n_lt = b * s // PEER_LT
    const2 = lambda i, j: (0, 0)
    tok3 = lambda i, j: (i, j, 0)
    lane_t = lambda i, j: (i * nt + j, 0, 0, 0)
    return pl.pallas_call(
        _merge_kernel,
        grid=(b, nt),
        in_specs=[pl.BlockSpec((1, ts, D_MODEL), tok3),
                  pl.BlockSpec((1, ts, D_MODEL), tok3),
                  pl.BlockSpec((1, ts, D_MODEL), tok3),
                  pl.BlockSpec((1, ts, 2 * D_MODEL), tok3),
                  pl.BlockSpec((1, 6, D_MODEL), lambda i, j: (i, 0, 0)),
                  pl.BlockSpec((D_MODEL, D_MODEL), const2),
                  pl.BlockSpec((1, D_MODEL), const2),
                  pl.BlockSpec((1, D_MODEL), const2),
                  pl.BlockSpec((D_MODEL, PEER_HEADS * 2 * PEER_HALF), const2),
                  pl.BlockSpec((2, N_KEYS, PEER_HALF), lambda i, j: (0, 0, 0))],
        out_specs=[pl.BlockSpec((1, ts, D_MODEL), tok3),
                   pl.BlockSpec((1, ts, D_MODEL), tok3),
                   pl.BlockSpec((ts // PEER_LT, PEER_HEADS, N_KEYS, PEER_LT), lane_t),
                   pl.BlockSpec((ts // PEER_LT, PEER_HEADS, N_KEYS, PEER_LT), lane_t),
                   pl.BlockSpec((ts // PEER_LT, N_KEYS, PEER_HEADS, PEER_LT), lane_t),
                   pl.BlockSpec((ts // PEER_LT, N_KEYS, PEER_HEADS, PEER_LT), lane_t)],
        out_shape=[jax.ShapeDtypeStruct((b, s, D_MODEL), F32),
                   jax.ShapeDtypeStruct((b, s, D_MODEL), BF16),
                   jax.ShapeDtypeStruct((n_lt, PEER_HEADS, N_KEYS, PEER_LT), BF16),
                   jax.ShapeDtypeStruct((n_lt, PEER_HEADS, N_KEYS, PEER_LT), BF16),
                   jax.ShapeDtypeStruct((n_lt, N_KEYS, PEER_HEADS, PEER_LT), BF16),
                   jax.ShapeDtypeStruct((n_lt, N_KEYS, PEER_HEADS, PEER_LT), BF16)],
        compiler_params=pltpu.CompilerParams(vmem_limit_bytes=VMEM_LIMIT),
        name="merge",
    )(x, attn, ypool, gates, mod, w_out, g_post, g_ffn, wq, keys)


def _peer_kernel(hf_ref, u_ref, vt_ref, rk_ref, e2_ref, cnt_ref, e1_ref, xmid_ref, mod_ref, gpost_ref,
                 o_ref, acc_ref, act_ref, a_ref):
    e = pl.program_id(2)
    n_lt = hf_ref.shape[1] // PEER_LT

    @pl.when(e == 0)
    def _():
        acc_ref[...] = jnp.zeros_like(acc_ref)

    for lt in range(n_lt):
        hf_blk = hf_ref[0, lt * PEER_LT:(lt + 1) * PEER_LT, :]
        act_ref[lt] = lax.dot_general(u_ref[...], hf_blk, _NT, preferred_element_type=F32)

    for lt in range(n_lt):
        for i in range(EI_PEER):
            cnt_all = cnt_ref[lt, i]
            e1_all = e1_ref[lt, i]
            w = jnp.zeros((N_KEYS, PEER_LT), BF16)
            for hd in range(PEER_HEADS):
                e2 = e2_ref[lt, hd]
                hit = rk_ref[lt, hd] < cnt_all[hd:hd + 1]
                w = w + jnp.where(hit, e2, jnp.zeros_like(e2)) * e1_all[hd:hd + 1]
            rows = slice(i * N_KEYS, (i + 1) * N_KEYS)
            a = act_ref[lt, rows, :]
            gelu = 0.5 * a * (1.0 + lax.erf(a * 0.7071067811865476))
            a_ref[lt, rows, :] = w * gelu.astype(BF16)

    for lt in range(n_lt):
        acc_ref[lt] += jnp.dot(vt_ref[...], a_ref[lt], preferred_element_type=F32)

    @pl.when(e == pl.num_programs(2) - 1)
    def _():
        m = mod_ref[0]
        for lt in range(n_lt):
            tok = slice(lt * PEER_LT, (lt + 1) * PEER_LT)
            y = acc_ref[lt].T
            o_ref[0, tok, :] = xmid_ref[0, tok, :] + m[5:6] * (_rms(y) * gpost_ref[...])


def _peer_call(hf, u_b, vt_b, s2t, e2t, cutt, e1t, x_mid, mod, g_post):
    b, s, _ = hf.shape
    tt = min(TT_PEER, s)
    nt = s // tt
    et = EI_PEER * N_KEYS
    ne = N_KEYS // EI_PEER
    tok3 = lambda i, j, e: (i, j, 0)
    lane_t = lambda i, j, e: (i * nt + j, 0, 0, 0)
    return pl.pallas_call(
        _peer_kernel,
        grid=(b, nt, ne),
        in_specs=[pl.BlockSpec((1, tt, D_MODEL), tok3),
                  pl.BlockSpec((et, D_MODEL), lambda i, j, e: (e, 0)),
                  pl.BlockSpec((D_MODEL, et), lambda i, j, e: (0, e)),
                  pl.BlockSpec((tt // PEER_LT, PEER_HEADS, N_KEYS, PEER_LT), lane_t),
                  pl.BlockSpec((tt // PEER_LT, PEER_HEADS, N_KEYS, PEER_LT), lane_t),
                  pl.BlockSpec((tt // PEER_LT, EI_PEER, PEER_HEADS, PEER_LT), lambda i, j, e: (i * nt + j, e, 0, 0)),
                  pl.BlockSpec((tt // PEER_LT, EI_PEER, PEER_HEADS, PEER_LT), lambda i, j, e: (i * nt + j, e, 0, 0)),
                  pl.BlockSpec((1, tt, D_MODEL), tok3),
                  pl.BlockSpec((1, 6, D_MODEL), lambda i, j, e: (i, 0, 0)),
                  pl.BlockSpec((1, D_MODEL), lambda i, j, e: (0, 0))],
        out_specs=pl.BlockSpec((1, tt, D_MODEL), tok3),
        out_shape=jax.ShapeDtypeStruct((b, s, D_MODEL), F32),
        scratch_shapes=[pltpu.VMEM((tt // PEER_LT, D_MODEL, PEER_LT), F32),
                        pltpu.VMEM((tt // PEER_LT, et, PEER_LT), F32),
                        pltpu.VMEM((tt // PEER_LT, et, PEER_LT), BF16)],
        compiler_params=pltpu.CompilerParams(
            vmem_limit_bytes=VMEM_LIMIT,
            dimension_semantics=("arbitrary", "arbitrary", "arbitrary")),
        name="peer",
    )(hf, u_b, vt_b, s2t, e2t, cutt, e1t, x_mid, mod, g_post)


def _rope_tables(seq):
    rows = seq // GRID_W
    row = jnp.broadcast_to(jnp.arange(rows, dtype=F32)[:, None], (rows, GRID_W)).reshape(-1)
    col = jnp.broadcast_to(jnp.arange(GRID_W, dtype=F32)[None, :], (rows, GRID_W)).reshape(-1)
    n_freq = ROPE // 4
    inv_freq = ROPE_BASE ** (-jnp.arange(n_freq, dtype=F32) / n_freq)
    ar = row[:, None] * inv_freq
    ac = col[:, None] * inv_freq
    cos = jnp.concatenate([jnp.cos(ar), jnp.cos(ar), jnp.cos(ac), jnp.cos(ac)], axis=1)
    sin = jnp.concatenate([-jnp.sin(ar), jnp.sin(ar), -jnp.sin(ac), jnp.sin(ac)], axis=1)
    return cos, sin


def _swap_halves(w):
    q = ROPE // 4
    return jnp.concatenate([w[..., q:2 * q], w[..., 0:q], w[..., 3 * q:4 * q], w[..., 2 * q:3 * q]], axis=-1)


def kernel(x, c, ctx, c_ctx, w_ada, b_ada, g_mix_pre, g_mix_post, g_ffn_pre, g_ffn_post, w_in, g_q_lora, w_uq,
           g_kv_lora, w_ukv, w_pool, pool_scale, w_out, peer_wq, peer_keys, peer_u, peer_v):
    assert w_ada.shape[0] == 1, "single-layer block"
    b, s, _ = x.shape
    layer = 0

    w_in_l = w_in[layer]
    k_rope_w = w_in_l[:, KV_LORA:KV_IN]
    win_r = jnp.concatenate([w_in_l[:, 0:KV_IN], _swap_halves(k_rope_w), w_in_l[:, KV_IN:]], axis=1).astype(BF16)
    wuq_h = w_uq[layer].reshape(Q_LORA, N_HEADS, QK_HEAD)
    wuq_r = jnp.concatenate([wuq_h, _swap_halves(wuq_h[..., NOPE:])], axis=-1).reshape(Q_LORA, N_HEADS * QH_W)
    wuq_r = wuq_r.astype(BF16)
    wukv_h = w_ukv[layer].reshape(KV_LORA, N_HEADS, NOPE + VHEAD)
    wk_abs = jnp.transpose(wukv_h[..., 0:NOPE], (1, 2, 0)).astype(BF16)
    wv_up = jnp.transpose(wukv_h[..., NOPE:], (1, 0, 2)).astype(BF16)
    cos_t, sin_t = _rope_tables(s)
    u_b = peer_u[layer].astype(BF16)
    vt_b = peer_v[layer].T.astype(BF16)
    row = lambda v: v[layer][None, :]

    n_rows = -(-(b + 1) // 8) * 8
    cc = jnp.concatenate([c, c_ctx[None, :], jnp.zeros((n_rows - b - 1, D_MODEL), F32)], axis=0)
    mod_all = _mod_call(cc, w_ada[layer], b_ada[layer][None, :])
    mod = mod_all[:b].reshape(b, 6, D_MODEL)
    mod_c = mod_all[b].reshape(6, D_MODEL)

    k_ctx = _ctx_call(ctx, mod_c, row(g_mix_pre), win_r[:, 0:256], row(g_kv_lora))
    k_lat, q, pool_in, gates = _proj_call(x, mod, row(g_mix_pre), win_r, row(g_q_lora), wuq_r, row(g_kv_lora),
                                          wk_abs, cos_t, sin_t)
    k_all = jnp.concatenate([k_ctx, k_lat], axis=1)
    attn = _attn_call(q, k_all, wv_up)
    ypool = _pool_call(pool_in, w_pool[layer].astype(BF16), row(pool_scale))
    x_mid, hf, s2t, e2t, cutt, e1t = _merge_call(
        x, attn, ypool, gates, mod, w_out[layer].astype(BF16), row(g_mix_post), row(g_ffn_pre),
        peer_wq[layer].astype(BF16), peer_keys[layer].astype(BF16))
    return _peer_call(hf, u_b, vt_b, s2t, e2t, cutt, e1t, x_mid, mod, row(g_ffn_post))
```

```python
import functools

import jax
import jax.numpy as jnp
from jax import lax
from jax.experimental import pallas as pl
from jax.experimental.pallas import tpu as pltpu

F32 = jnp.float32
BF16 = jnp.bfloat16

D_MODEL = 1024
GRID_W = 64
N_HEADS = 8
NOPE = 128
ROPE = 64
VHEAD = 128
QK_HEAD = NOPE + ROPE
Q_LORA = 256
KV_LORA = 128
ROPE_BASE = 10000.0
ATTN_SCALE = QK_HEAD ** -0.5
POOL_GROUPS = 4
POOL_GROUP_DIM = D_MODEL // POOL_GROUPS
POOL_MAX_HALF = 1 << (POOL_GROUPS - 1)
KV_IN = KV_LORA + ROPE
PEER_HEADS = 8
N_KEYS = 128
PEER_HALF = 128
PEER_TOPK = 16
EPS = 1e-6

WIN_SMALL = 512
WIN_POOL = WIN_SMALL
WIN_GATE = WIN_POOL + D_MODEL
WIN_TOTAL = WIN_GATE + 2 * D_MODEL
QH_W = 256

LANE = 128
PEER_LT = 256
VMEM_LIMIT = 56 * 1024 * 1024

TS_PROJ = 512
TQ_ATTN = 256
TS_MERGE = 256
TT_PEER = 1024
EI_PEER = 8

_NT = (((1,), (1,)), ((), ()))


def _rms(x):
    return x * lax.rsqrt(jnp.mean(x * x, axis=-1, keepdims=True) + EPS)


def _mod_kernel(cc_ref, w_ref, b_ref, o_ref):
    cc = cc_ref[...]
    a = cc * jax.nn.sigmoid(cc)
    o_ref[...] = jnp.dot(a, w_ref[...], preferred_element_type=F32) + b_ref[...]


def _mod_call(cc, w_ada, b_ada):
    rows = cc.shape[0]
    n = w_ada.shape[1]
    bn = 1024
    return pl.pallas_call(
        _mod_kernel,
        grid=(n // bn,),
        in_specs=[pl.BlockSpec((rows, D_MODEL), lambda j: (0, 0)),
                  pl.BlockSpec((D_MODEL, bn), lambda j: (0, j)),
                  pl.BlockSpec((1, bn), lambda j: (0, j))],
        out_specs=pl.BlockSpec((rows, bn), lambda j: (0, j)),
        out_shape=jax.ShapeDtypeStruct((rows, n), F32),
        compiler_params=pltpu.CompilerParams(vmem_limit_bytes=VMEM_LIMIT),
        name="mod",
    )(cc, w_ada, b_ada)


def _ctx_kernel(ctx_ref, mod_ref, gpre_ref, win_ref, gkv_ref, o_ref):
    x = ctx_ref[0]
    m = mod_ref[...]
    h = _rms(x) * gpre_ref[...]
    h = h * (1.0 + m[1:2]) + m[0:1]
    p = jnp.dot(h.astype(BF16), win_ref[...], preferred_element_type=F32)
    ckv_n = _rms(p[:, :KV_LORA]) * gkv_ref[...]
    o_ref[0, :, 0:KV_LORA] = ckv_n.astype(BF16)
    o_ref[0, :, KV_LORA:KV_IN] = p[:, KV_LORA:KV_IN].astype(BF16)


def _ctx_call(ctx, mod_c, g_pre, win_r, g_kv):
    b, lc, _ = ctx.shape
    return pl.pallas_call(
        _ctx_kernel,
        grid=(b,),
        in_specs=[pl.BlockSpec((1, lc, D_MODEL), lambda i: (i, 0, 0)),
                  pl.BlockSpec((6, D_MODEL), lambda i: (0, 0)),
                  pl.BlockSpec((1, D_MODEL), lambda i: (0, 0)),
                  pl.BlockSpec((D_MODEL, 256), lambda i: (0, 0)),
                  pl.BlockSpec((1, KV_LORA), lambda i: (0, 0))],
        out_specs=pl.BlockSpec((1, lc, KV_IN), lambda i: (i, 0, 0)),
        out_shape=jax.ShapeDtypeStruct((b, lc, KV_IN), BF16),
        compiler_params=pltpu.CompilerParams(vmem_limit_bytes=VMEM_LIMIT),
        name="ctx_kv",
    )(ctx, mod_c, g_pre, win_r, g_kv)


def _proj_kernel(x_ref, mod_ref, gpre_ref, win_ref, gq_ref, wuq_ref, gkv_ref, wk_ref, cos_ref, sin_ref,
                 kcat_ref, q_ref, pool_ref, gates_ref):
    x = x_ref[0]
    m = mod_ref[0]
    h = _rms(x) * gpre_ref[...]
    hb = (h * (1.0 + m[1:2]) + m[0:1]).astype(BF16)
    cos = cos_ref[...]
    sin = sin_ref[...]

    p0 = jnp.dot(hb, win_ref[:, 0:WIN_SMALL], preferred_element_type=F32)
    ckv_n = _rms(p0[:, 0:KV_LORA]) * gkv_ref[...]
    k_rot = p0[:, 128:192] * cos + p0[:, 192:256] * sin
    kcat_ref[0, :, 0:KV_LORA] = ckv_n.astype(BF16)
    kcat_ref[0, :, KV_LORA:KV_IN] = k_rot.astype(BF16)

    cqn = (_rms(p0[:, 256:512]) * gq_ref[...]).astype(BF16)
    for hd in range(N_HEADS):
        qh = jnp.dot(cqn, wuq_ref[:, hd * QH_W:(hd + 1) * QH_W], preferred_element_type=F32)
        q_abs = jnp.dot(qh[:, 0:NOPE].astype(BF16), wk_ref[hd], preferred_element_type=F32)
        q_rot = qh[:, 128:192] * cos + qh[:, 192:256] * sin
        q_ref[0, hd, :, 0:KV_LORA] = (q_abs * ATTN_SCALE).astype(BF16)
        q_ref[0, hd, :, KV_LORA:KV_IN] = (q_rot * ATTN_SCALE).astype(BF16)

    pool_ref[0] = jnp.dot(hb, win_ref[:, WIN_POOL:WIN_GATE], preferred_element_type=F32).astype(BF16)
    for half in range(2):
        lo = WIN_GATE + half * D_MODEL
        g = jnp.dot(hb, win_ref[:, lo:lo + D_MODEL], preferred_element_type=F32)
        gates_ref[0, :, half * D_MODEL:(half + 1) * D_MODEL] = jax.nn.sigmoid(g).astype(BF16)


def _proj_call(x, mod, g_pre, win_r, g_q, wuq_r, g_kv, wk_abs, cos_t, sin_t):
    b, s, _ = x.shape
    ts = min(TS_PROJ, s)
    const2 = lambda i, j: (0, 0)
    return pl.pallas_call(
        _proj_kernel,
        grid=(b, s // ts),
        in_specs=[pl.BlockSpec((1, ts, D_MODEL), lambda i, j: (i, j, 0)),
                  pl.BlockSpec((1, 6, D_MODEL), lambda i, j: (i, 0, 0)),
                  pl.BlockSpec((1, D_MODEL), const2),
                  pl.BlockSpec((D_MODEL, WIN_TOTAL), const2),
                  pl.BlockSpec((1, Q_LORA), const2),
                  pl.BlockSpec((Q_LORA, N_HEADS * QH_W), const2),
                  pl.BlockSpec((1, KV_LORA), const2),
                  pl.BlockSpec((N_HEADS, NOPE, KV_LORA), lambda i, j: (0, 0, 0)),
                  pl.BlockSpec((ts, ROPE), lambda i, j: (j, 0)),
                  pl.BlockSpec((ts, ROPE), lambda i, j: (j, 0))],
        out_specs=[pl.BlockSpec((1, ts, KV_IN), lambda i, j: (i, j, 0)),
                   pl.BlockSpec((1, N_HEADS, ts, KV_IN), lambda i, j: (i, 0, j, 0)),
                   pl.BlockSpec((1, ts, D_MODEL), lambda i, j: (i, j, 0)),
                   pl.BlockSpec((1, ts, 2 * D_MODEL), lambda i, j: (i, j, 0))],
        out_shape=[jax.ShapeDtypeStruct((b, s, KV_IN), BF16),
                   jax.ShapeDtypeStruct((b, N_HEADS, s, KV_IN), BF16),
                   jax.ShapeDtypeStruct((b, s, D_MODEL), BF16),
                   jax.ShapeDtypeStruct((b, s, 2 * D_MODEL), BF16)],
        compiler_params=pltpu.CompilerParams(vmem_limit_bytes=VMEM_LIMIT),
        name="proj",
    )(x, mod, g_pre, win_r, g_q, wuq_r, g_kv, wk_abs, cos_t, sin_t)


def _attn_kernel(q_ref, k_ref, wv_ref, o_ref):
    k = k_ref[0]
    v = k[:, 0:KV_LORA]
    for hd in range(N_HEADS):
        s = lax.dot_general(q_ref[0, hd], k, _NT, preferred_element_type=F32)
        p = jnp.exp(s - jnp.max(s, axis=-1, keepdims=True))
        l = jnp.sum(p, axis=-1, keepdims=True)
        o = jnp.dot(p.astype(BF16), v, preferred_element_type=F32) / l
        a = jnp.dot(o.astype(BF16), wv_ref[hd], preferred_element_type=F32)
        o_ref[0, :, hd * VHEAD:(hd + 1) * VHEAD] = a.astype(BF16)


def _attn_call(q, kall, wv):
    b, _, s, _ = q.shape
    lk = kall.shape[1]
    tq = min(TQ_ATTN, s)
    return pl.pallas_call(
        _attn_kernel,
        grid=(b, s // tq),
        in_specs=[pl.BlockSpec((1, N_HEADS, tq, KV_IN), lambda i, j: (i, 0, j, 0)),
                  pl.BlockSpec((1, lk, KV_IN), lambda i, j: (i, 0, 0)),
                  pl.BlockSpec((N_HEADS, KV_LORA, VHEAD), lambda i, j: (0, 0, 0))],
        out_specs=pl.BlockSpec((1, tq, D_MODEL), lambda i, j: (i, j, 0)),
        out_shape=jax.ShapeDtypeStruct((b, s, D_MODEL), BF16),
        compiler_params=pltpu.CompilerParams(vmem_limit_bytes=VMEM_LIMIT),
        name="attn",
    )(q, kall, wv)


def _pool_kernel(xp_ref, w_ref, ps_ref, o_ref):
    g = pl.program_id(1)
    x = xp_ref[0].astype(F32)
    n = x.shape[0]
    t = lax.broadcasted_iota(jnp.int32, x.shape, 0)
    pad = POOL_MAX_HALF
    zeros = jnp.zeros((pad, x.shape[1]), F32)
    xe = jnp.concatenate([zeros, x, zeros], axis=0)
    ne = n + 2 * pad

    def down(a, k):
        return pltpu.roll(a, k, 0)

    def up(a, k):
        return pltpu.roll(a, ne - k, 0)

    w = down(xe, 1) + xe
    for lvl in range(1, POOL_GROUPS):
        k = 1 << (lvl - 1)
        w = jnp.where(g >= lvl, down(w, k) + up(w, k), w)
    w = w[pad:pad + n]
    half = lax.shift_left(jnp.int32(1), g)
    count = (jnp.minimum(t + half, n) - jnp.maximum(t - half, 0)).astype(F32)
    pooled = (w / count - x).astype(BF16)
    y = jnp.dot(pooled, w_ref[0], preferred_element_type=F32) * ps_ref[...]
    o_ref[0] = y.astype(BF16)


def _pool_call(pool_in, w_pool, pool_scale):
    b, s, _ = pool_in.shape
    gd = POOL_GROUP_DIM
    return pl.pallas_call(
        _pool_kernel,
        grid=(b, POOL_GROUPS),
        in_specs=[pl.BlockSpec((1, s, gd), lambda i, g: (i, 0, g)),
                  pl.BlockSpec((1, gd, gd), lambda i, g: (g, 0, 0)),
                  pl.BlockSpec((1, gd), lambda i, g: (0, g))],
        out_specs=pl.BlockSpec((1, s, gd), lambda i, g: (i, 0, g)),
        out_shape=jax.ShapeDtypeStruct((b, s, D_MODEL), BF16),
        compiler_params=pltpu.CompilerParams(vmem_limit_bytes=VMEM_LIMIT),
        name="pool",
    )(pool_in, w_pool, pool_scale)


SUB = 8


def _sort_network(n):
    pairs = []
    p = 1
    while p < PEER_TOPK:
        k = p
        while k >= 1:
            for j in range(k % p, PEER_TOPK - k, 2 * k):
                for i in range(min(k, PEER_TOPK - j - k)):
                    if (i + j) // (2 * p) == (i + j + k) // (2 * p):
                        pairs.append((i + j, i + j + k))
            k //= 2
        p *= 2
    return [(i, j) for i, j in pairs if j < n]


def _top16_sorted(vs):
    vs = list(vs)
    for i, j in _sort_network(len(vs)):
        vs[i], vs[j] = jnp.maximum(vs[i], vs[j]), jnp.minimum(vs[i], vs[j])
    vs += [jnp.full(vs[0].shape, -jnp.inf, F32)] * (PEER_TOPK - len(vs))
    shift = SUB // 2
    while shift >= 1:
        other = [pltpu.roll(v, shift, 0) for v in vs]
        vs = [jnp.maximum(vs[k], other[PEER_TOPK - 1 - k]) for k in range(PEER_TOPK)]
        d = PEER_TOPK // 2
        while d >= 1:
            for i in range(PEER_TOPK):
                if i & d == 0:
                    vs[i], vs[i + d] = jnp.maximum(vs[i], vs[i + d]), jnp.minimum(vs[i], vs[i + d])
            d //= 2
        shift //= 2
    return vs


def _peer_stats(s1, s2):
    lanes = s1.shape[1]
    n_v = N_KEYS // SUB
    rows = lax.broadcasted_iota(jnp.int32, (SUB, lanes), 0)
    v1 = [s1[SUB * k:SUB * (k + 1)] for k in range(n_v)]
    v2 = [s2[SUB * k:SUB * (k + 1)] for k in range(n_v)]
    a = _top16_sorted(v1)
    b = _top16_sorted(v2)

    def by_rank(vals):
        out = vals[0]
        for r in range(1, SUB):
            out = jnp.where(rows == r, vals[r], out)
        return out

    a_lo, a_hi, b_hi = by_rank(a[0:SUB]), by_rank(a[SUB:]), by_rank(b[SUB:])
    pieces = [a_lo + b[0], a_hi + b[0]] + [a_lo + b[c] for c in range(1, SUB)] + [b_hi + a[0]]
    thr = _top16_sorted(pieces)[PEER_TOPK - 1]
    top = a[0] + b[0]
    sel = [p >= thr for p in pieces]
    z = sum(jnp.where(m, jnp.exp(p - top), 0.0) for m, p in zip(sel, pieces))
    inv_z = 1.0 / jnp.broadcast_to(jnp.sum(z, axis=0, keepdims=True), (SUB, lanes))
    one = [jnp.where(m, 1.0, 0.0) for m in sel]
    n_hi = one[1]
    n_lo = one[0] + sum(one[2:2 + SUB - 1])
    extra = jnp.broadcast_to(jnp.sum(one[-1], axis=0, keepdims=True), (SUB, lanes))
    n_lo = n_lo + jnp.where(rows == 0, extra, 0.0)
    n_of = [jnp.broadcast_to((n_lo if r < SUB else n_hi)[r % SUB:r % SUB + 1], (SUB, lanes))
            for r in range(PEER_TOPK)]
    cnt, e1, rank2, e2 = [], [], [], []
    for k in range(n_v):
        c = jnp.zeros((SUB, lanes), F32)
        rk = jnp.zeros((SUB, lanes), F32)
        for r in range(PEER_TOPK):
            c = jnp.where(v1[k] == a[r], n_of[r], c)
            rk = jnp.where(v2[k] < b[r], float(r + 1), rk)
        cnt.append(c)
        rank2.append(rk)
        e1.append(jnp.exp(v1[k] - a[0]) * inv_z)
        e2.append(jnp.exp(v2[k] - b[0]))
    cat = lambda vs: jnp.concatenate(vs, axis=0)
    return cat(cnt), cat(e1), cat(rank2), cat(e2)


def _merge_kernel(x_ref, attn_ref, yp_ref, gates_ref, mod_ref, wout_ref, gpost_ref, gffn_ref, wq_ref, keys_ref,
                  xmid_ref, hf_ref, rk_ref, e2_ref, cnt_ref, e1_ref):
    m = mod_ref[0]
    ga = gates_ref[0, :, 0:D_MODEL].astype(F32)
    gb = gates_ref[0, :, D_MODEL:2 * D_MODEL].astype(F32)
    merged = ga * attn_ref[0].astype(F32) + gb * yp_ref[0].astype(F32)
    y = jnp.dot(merged.astype(BF16), wout_ref[...], preferred_element_type=F32)
    x_mid = x_ref[0] + m[2:3] * (_rms(y) * gpost_ref[...])
    xmid_ref[0] = x_mid
    hf = (_rms(x_mid) * gffn_ref[...]) * (1.0 + m[4:5]) + m[3:4]
    hfb = hf.astype(BF16)
    hf_ref[0] = hfb
    q = jnp.dot(hfb, wq_ref[...], preferred_element_type=F32).astype(BF16)
    k1 = keys_ref[0]
    k2 = keys_ref[1]
    ts = q.shape[0]
    for hd in range(PEER_HEADS):
        base = hd * 2 * PEER_HALF
        s1 = lax.dot_general(k1, q[:, base:base + PEER_HALF], _NT, preferred_element_type=F32)
        s2 = lax.dot_general(k2, q[:, base + PEER_HALF:base + 2 * PEER_HALF], _NT, preferred_element_type=F32)
        for lt in range(ts // LANE):
            sl = slice(lt * LANE, (lt + 1) * LANE)
            blk, off = divmod(lt * LANE, PEER_LT)
            dst = slice(off, off + LANE)
            cnt, e1, rank2, e2 = _peer_stats(s1[:, sl], s2[:, sl])
            rk_ref[blk, hd, :, dst] = rank2.astype(BF16)
            e2_ref[blk, hd, :, dst] = e2.astype(BF16)
            cnt_ref[blk, hd, :, dst] = cnt
            e1_ref[blk, hd, :, dst] = e1


def _merge_call(x, attn, ypool, gates, mod, w_out, g_post, g_ffn, wq, keys):
    b, s, _ = x.shape
    ts = min(TS_MERGE, s)
    nt = s // ts
    n_lt = b * s // PEER_LT
    const2 = lambda i, j: (0, 0)
    tok3 = lambda i, j: (i, j, 0)
    lane_t = lambda i, j: (i * nt + j, 0, 0, 0)
    return pl.pallas_call(
        _merge_kernel,
        grid=(b, nt),
        in_specs=[pl.BlockSpec((1, ts, D_MODEL), tok3),
                  pl.BlockSpec((1, ts, D_MODEL), tok3),
                  pl.BlockSpec((1, ts, D_MODEL), tok3),
                  pl.BlockSpec((1, ts, 2 * D_MODEL), tok3),
                  pl.BlockSpec((1, 6, D_MODEL), lambda i, j: (i, 0, 0)),
                  pl.BlockSpec((D_MODEL, D_MODEL), const2),
                  pl.BlockSpec((1, D_MODEL), const2),
                  pl.BlockSpec((1, D_MODEL), const2),
                  pl.BlockSpec((D_MODEL, PEER_HEADS * 2 * PEER_HALF), const2),
                  pl.BlockSpec((2, N_KEYS, PEER_HALF), lambda i, j: (0, 0, 0))],
        out_specs=[pl.BlockSpec((1, ts, D_MODEL), tok3),
                   pl.BlockSpec((1, ts, D_MODEL), tok3),
                   pl.BlockSpec((ts // PEER_LT, PEER_HEADS, N_KEYS, PEER_LT), lane_t),
                   pl.BlockSpec((ts // PEER_LT, PEER_HEADS, N_KEYS, PEER_LT), lane_t),
                   pl.BlockSpec((ts // PEER_LT, PEER_HEADS, N_KEYS, PEER_LT), lane_t),
                   pl.BlockSpec((ts // PEER_LT, PEER_HEADS, N_KEYS, PEER_LT), lane_t)],
        out_shape=[jax.ShapeDtypeStruct((b, s, D_MODEL), F32),
                   jax.ShapeDtypeStruct((b, s, D_MODEL), BF16),
                   jax.ShapeDtypeStruct((n_lt, PEER_HEADS, N_KEYS, PEER_LT), BF16),
                   jax.ShapeDtypeStruct((n_lt, PEER_HEADS, N_KEYS, PEER_LT), BF16),
                   jax.ShapeDtypeStruct((n_lt, PEER_HEADS, N_KEYS, PEER_LT), F32),
                   jax.ShapeDtypeStruct((n_lt, PEER_HEADS, N_KEYS, PEER_LT), F32)],
        compiler_params=pltpu.CompilerParams(vmem_limit_bytes=VMEM_LIMIT),
        name="merge",
    )(x, attn, ypool, gates, mod, w_out, g_post, g_ffn, wq, keys)


def _peer_kernel(hf_ref, u_ref, vt_ref, rk_ref, e2_ref, cnt_ref, e1_ref, xmid_ref, mod_ref, gpost_ref,
                 o_ref, acc_ref, act_ref, a_ref):
    e = pl.program_id(2)
    n_lt = hf_ref.shape[1] // PEER_LT

    @pl.when(e == 0)
    def _():
        acc_ref[...] = jnp.zeros_like(acc_ref)

    for lt in range(n_lt):
        hf_blk = hf_ref[0, lt * PEER_LT:(lt + 1) * PEER_LT, :]
        act_ref[lt] = lax.dot_general(u_ref[...], hf_blk, _NT, preferred_element_type=F32)

    for lt in range(n_lt):
        for i in range(EI_PEER):
            cnt_all = cnt_ref[lt, :, i, :].astype(BF16)
            e1_all = e1_ref[lt, :, i, :].astype(BF16)
            w = jnp.zeros((N_KEYS, PEER_LT), BF16)
            for hd in range(PEER_HEADS):
                e2 = e2_ref[lt, hd]
                hit = rk_ref[lt, hd] < cnt_all[hd:hd + 1]
                w = w + jnp.where(hit, e2, jnp.zeros_like(e2)) * e1_all[hd:hd + 1]
            rows = slice(i * N_KEYS, (i + 1) * N_KEYS)
            a = act_ref[lt, rows, :]
            gelu = 0.5 * a * (1.0 + lax.erf(a * 0.7071067811865476))
            a_ref[lt, rows, :] = w * gelu.astype(BF16)

    for lt in range(n_lt):
        acc_ref[lt] += jnp.dot(vt_ref[...], a_ref[lt], preferred_element_type=F32)

    @pl.when(e == pl.num_programs(2) - 1)
    def _():
        m = mod_ref[0]
        for lt in range(n_lt):
            tok = slice(lt * PEER_LT, (lt + 1) * PEER_LT)
            y = acc_ref[lt].T
            o_ref[0, tok, :] = xmid_ref[0, tok, :] + m[5:6] * (_rms(y) * gpost_ref[...])


def _peer_call(hf, u_b, vt_b, s2t, e2t, cutt, e1t, x_mid, mod, g_post):
    b, s, _ = hf.shape
    tt = min(TT_PEER, s)
    nt = s // tt
    et = EI_PEER * N_KEYS
    ne = N_KEYS // EI_PEER
    tok3 = lambda i, j, e: (i, j, 0)
    lane_t = lambda i, j, e: (i * nt + j, 0, 0, 0)
    return pl.pallas_call(
        _peer_kernel,
        grid=(b, nt, ne),
        in_specs=[pl.BlockSpec((1, tt, D_MODEL), tok3),
                  pl.BlockSpec((et, D_MODEL), lambda i, j, e: (e, 0)),
                  pl.BlockSpec((D_MODEL, et), lambda i, j, e: (0, e)),
                  pl.BlockSpec((tt // PEER_LT, PEER_HEADS, N_KEYS, PEER_LT), lane_t),
                  pl.BlockSpec((tt // PEER_LT, PEER_HEADS, N_KEYS, PEER_LT), lane_t),
                  pl.BlockSpec((tt // PEER_LT, PEER_HEADS, EI_PEER, PEER_LT), lambda i, j, e: (i * nt + j, 0, e, 0)),
                  pl.BlockSpec((tt // PEER_LT, PEER_HEADS, EI_PEER, PEER_LT), lambda i, j, e: (i * nt + j, 0, e, 0)),
                  pl.BlockSpec((1, tt, D_MODEL), tok3),
                  pl.BlockSpec((1, 6, D_MODEL), lambda i, j, e: (i, 0, 0)),
                  pl.BlockSpec((1, D_MODEL), lambda i, j, e: (0, 0))],
        out_specs=pl.BlockSpec((1, tt, D_MODEL), tok3),
        out_shape=jax.ShapeDtypeStruct((b, s, D_MODEL), F32),
        scratch_shapes=[pltpu.VMEM((tt // PEER_LT, D_MODEL, PEER_LT), F32),
                        pltpu.VMEM((tt // PEER_LT, et, PEER_LT), F32),
                        pltpu.VMEM((tt // PEER_LT, et, PEER_LT), BF16)],
        compiler_params=pltpu.CompilerParams(
            vmem_limit_bytes=VMEM_LIMIT,
            dimension_semantics=("arbitrary", "arbitrary", "arbitrary")),
        name="peer",
    )(hf, u_b, vt_b, s2t, e2t, cutt, e1t, x_mid, mod, g_post)


def _rope_tables(seq):
    rows = seq // GRID_W
    row = jnp.broadcast_to(jnp.arange(rows, dtype=F32)[:, None], (rows, GRID_W)).reshape(-1)
    col = jnp.broadcast_to(jnp.arange(GRID_W, dtype=F32)[None, :], (rows, GRID_W)).reshape(-1)
    n_freq = ROPE // 4
    inv_freq = ROPE_BASE ** (-jnp.arange(n_freq, dtype=F32) / n_freq)
    ar = row[:, None] * inv_freq
    ac = col[:, None] * inv_freq
    cos = jnp.concatenate([jnp.cos(ar), jnp.cos(ar), jnp.cos(ac), jnp.cos(ac)], axis=1)
    sin = jnp.concatenate([-jnp.sin(ar), jnp.sin(ar), -jnp.sin(ac), jnp.sin(ac)], axis=1)
    return cos, sin


def _swap_halves(w):
    q = ROPE // 4
    return jnp.concatenate([w[..., q:2 * q], w[..., 0:q], w[..., 3 * q:4 * q], w[..., 2 * q:3 * q]], axis=-1)


def kernel(x, c, ctx, c_ctx, w_ada, b_ada, g_mix_pre, g_mix_post, g_ffn_pre, g_ffn_post, w_in, g_q_lora, w_uq,
           g_kv_lora, w_ukv, w_pool, pool_scale, w_out, peer_wq, peer_keys, peer_u, peer_v):
    assert w_ada.shape[0] == 1, "single-layer block"
    b, s, _ = x.shape
    layer = 0

    w_in_l = w_in[layer]
    k_rope_w = w_in_l[:, KV_LORA:KV_IN]
    win_r = jnp.concatenate([w_in_l[:, 0:KV_IN], _swap_halves(k_rope_w), w_in_l[:, KV_IN:]], axis=1).astype(BF16)
    wuq_h = w_uq[layer].reshape(Q_LORA, N_HEADS, QK_HEAD)
    wuq_r = jnp.concatenate([wuq_h, _swap_halves(wuq_h[..., NOPE:])], axis=-1).reshape(Q_LORA, N_HEADS * QH_W)
    wuq_r = wuq_r.astype(BF16)
    wukv_h = w_ukv[layer].reshape(KV_LORA, N_HEADS, NOPE + VHEAD)
    wk_abs = jnp.transpose(wukv_h[..., 0:NOPE], (1, 2, 0)).astype(BF16)
    wv_up = jnp.transpose(wukv_h[..., NOPE:], (1, 0, 2)).astype(BF16)
    cos_t, sin_t = _rope_tables(s)
    u_b = peer_u[layer].astype(BF16)
    vt_b = peer_v[layer].T.astype(BF16)
    row = lambda v: v[layer][None, :]

    n_rows = -(-(b + 1) // 8) * 8
    cc = jnp.concatenate([c, c_ctx[None, :], jnp.zeros((n_rows - b - 1, D_MODEL), F32)], axis=0)
    mod_all = _mod_call(cc, w_ada[layer], b_ada[layer][None, :])
    mod = mod_all[:b].reshape(b, 6, D_MODEL)
    mod_c = mod_all[b].reshape(6, D_MODEL)

    k_ctx = _ctx_call(ctx, mod_c, row(g_mix_pre), win_r[:, 0:256], row(g_kv_lora))
    k_lat, q, pool_in, gates = _proj_call(x, mod, row(g_mix_pre), win_r, row(g_q_lora), wuq_r, row(g_kv_lora),
                                          wk_abs, cos_t, sin_t)
    k_all = jnp.concatenate([k_ctx, k_lat], axis=1)
    attn = _attn_call(q, k_all, wv_up)
    ypool = _pool_call(pool_in, w_pool[layer].astype(BF16), row(pool_scale))
    x_mid, hf, s2t, e2t, cutt, e1t = _merge_call(
        x, attn, ypool, gates, mod, w_out[layer].astype(BF16), row(g_mix_post), row(g_ffn_pre),
        peer_wq[layer].astype(BF16), peer_keys[layer].astype(BF16))
    return _peer_call(hf, u_b, vt_b, s2t, e2t, cutt, e1t, x_mid, mod, row(g_ffn_post))
```

```python
import functools

import jax
import jax.numpy as jnp
from jax import lax
from jax.experimental import pallas as pl
from jax.experimental.pallas import tpu as pltpu

F32 = jnp.float32
BF16 = jnp.bfloat16

D_MODEL = 1024
GRID_W = 64
N_HEADS = 8
NOPE = 128
ROPE = 64
VHEAD = 128
QK_HEAD = NOPE + ROPE
Q_LORA = 256
KV_LORA = 128
ROPE_BASE = 10000.0
ATTN_SCALE = QK_HEAD ** -0.5
POOL_GROUPS = 4
POOL_GROUP_DIM = D_MODEL // POOL_GROUPS
POOL_MAX_HALF = 1 << (POOL_GROUPS - 1)
KV_IN = KV_LORA + ROPE
PEER_HEADS = 8
N_KEYS = 128
PEER_HALF = 128
PEER_TOPK = 16
EPS = 1e-6

WIN_SMALL = 512
WIN_POOL = WIN_SMALL
WIN_GATE = WIN_POOL + D_MODEL
WIN_TOTAL = WIN_GATE + 2 * D_MODEL
QH_W = 256

LANE = 128
PEER_LT = 256
VMEM_LIMIT = 56 * 1024 * 1024

TS_PROJ = 512
TQ_ATTN = 256
TS_MERGE = 256
TT_PEER = 1024
EI_PEER = 8

_NT = (((1,), (1,)), ((), ()))


def _rms(x):
    return x * lax.rsqrt(jnp.mean(x * x, axis=-1, keepdims=True) + EPS)


def _mod_kernel(cc_ref, w_ref, b_ref, o_ref):
    cc = cc_ref[...]
    a = cc * jax.nn.sigmoid(cc)
    o_ref[...] = jnp.dot(a, w_ref[...], preferred_element_type=F32) + b_ref[...]


def _mod_call(cc, w_ada, b_ada):
    rows = cc.shape[0]
    n = w_ada.shape[1]
    bn = 1024
    return pl.pallas_call(
        _mod_kernel,
        grid=(n // bn,),
        in_specs=[pl.BlockSpec((rows, D_MODEL), lambda j: (0, 0)),
                  pl.BlockSpec((D_MODEL, bn), lambda j: (0, j)),
                  pl.BlockSpec((1, bn), lambda j: (0, j))],
        out_specs=pl.BlockSpec((rows, bn), lambda j: (0, j)),
        out_shape=jax.ShapeDtypeStruct((rows, n), F32),
        compiler_params=pltpu.CompilerParams(vmem_limit_bytes=VMEM_LIMIT),
        name="mod",
    )(cc, w_ada, b_ada)


def _ctx_kernel(ctx_ref, mod_ref, gpre_ref, win_ref, gkv_ref, o_ref):
    x = ctx_ref[0]
    m = mod_ref[...]
    h = _rms(x) * gpre_ref[...]
    h = h * (1.0 + m[1:2]) + m[0:1]
    p = jnp.dot(h.astype(BF16), win_ref[...], preferred_element_type=F32)
    ckv_n = _rms(p[:, :KV_LORA]) * gkv_ref[...]
    o_ref[0, :, 0:KV_LORA] = ckv_n.astype(BF16)
    o_ref[0, :, KV_LORA:KV_IN] = p[:, KV_LORA:KV_IN].astype(BF16)


def _ctx_call(ctx, mod_c, g_pre, win_r, g_kv):
    b, lc, _ = ctx.shape
    return pl.pallas_call(
        _ctx_kernel,
        grid=(b,),
        in_specs=[pl.BlockSpec((1, lc, D_MODEL), lambda i: (i, 0, 0)),
                  pl.BlockSpec((6, D_MODEL), lambda i: (0, 0)),
                  pl.BlockSpec((1, D_MODEL), lambda i: (0, 0)),
                  pl.BlockSpec((D_MODEL, 256), lambda i: (0, 0)),
                  pl.BlockSpec((1, KV_LORA), lambda i: (0, 0))],
        out_specs=pl.BlockSpec((1, lc, KV_IN), lambda i: (i, 0, 0)),
        out_shape=jax.ShapeDtypeStruct((b, lc, KV_IN), BF16),
        compiler_params=pltpu.CompilerParams(vmem_limit_bytes=VMEM_LIMIT),
        name="ctx_kv",
    )(ctx, mod_c, g_pre, win_r, g_kv)


def _proj_kernel(x_ref, mod_ref, gpre_ref, win_ref, gq_ref, wuq_ref, gkv_ref, wk_ref, cos_ref, sin_ref,
                 kcat_ref, q_ref, pool_ref, gates_ref):
    x = x_ref[0]
    m = mod_ref[0]
    h = _rms(x) * gpre_ref[...]
    hb = (h * (1.0 + m[1:2]) + m[0:1]).astype(BF16)
    cos = cos_ref[...]
    sin = sin_ref[...]

    p0 = jnp.dot(hb, win_ref[:, 0:WIN_SMALL], preferred_element_type=F32)
    ckv_n = _rms(p0[:, 0:KV_LORA]) * gkv_ref[...]
    k_rot = p0[:, 128:192] * cos + p0[:, 192:256] * sin
    kcat_ref[0, :, 0:KV_LORA] = ckv_n.astype(BF16)
    kcat_ref[0, :, KV_LORA:KV_IN] = k_rot.astype(BF16)

    cqn = (_rms(p0[:, 256:512]) * gq_ref[...]).astype(BF16)
    for hd in range(N_HEADS):
        qh = jnp.dot(cqn, wuq_ref[:, hd * QH_W:(hd + 1) * QH_W], preferred_element_type=F32)
        q_abs = jnp.dot(qh[:, 0:NOPE].astype(BF16), wk_ref[hd], preferred_element_type=F32)
        q_rot = qh[:, 128:192] * cos + qh[:, 192:256] * sin
        q_ref[0, hd, :, 0:KV_LORA] = (q_abs * ATTN_SCALE).astype(BF16)
        q_ref[0, hd, :, KV_LORA:KV_IN] = (q_rot * ATTN_SCALE).astype(BF16)

    pool_ref[0] = jnp.dot(hb, win_ref[:, WIN_POOL:WIN_GATE], preferred_element_type=F32).astype(BF16)
    for half in range(2):
        lo = WIN_GATE + half * D_MODEL
        g = jnp.dot(hb, win_ref[:, lo:lo + D_MODEL], preferred_element_type=F32)
        gates_ref[0, :, half * D_MODEL:(half + 1) * D_MODEL] = jax.nn.sigmoid(g).astype(BF16)


def _proj_call(x, mod, g_pre, win_r, g_q, wuq_r, g_kv, wk_abs, cos_t, sin_t):
    b, s, _ = x.shape
    ts = min(TS_PROJ, s)
    const2 = lambda i, j: (0, 0)
    return pl.pallas_call(
        _proj_kernel,
        grid=(b, s // ts),
        in_specs=[pl.BlockSpec((1, ts, D_MODEL), lambda i, j: (i, j, 0)),
                  pl.BlockSpec((1, 6, D_MODEL), lambda i, j: (i, 0, 0)),
                  pl.BlockSpec((1, D_MODEL), const2),
                  pl.BlockSpec((D_MODEL, WIN_TOTAL), const2),
                  pl.BlockSpec((1, Q_LORA), const2),
                  pl.BlockSpec((Q_LORA, N_HEADS * QH_W), const2),
                  pl.BlockSpec((1, KV_LORA), const2),
                  pl.BlockSpec((N_HEADS, NOPE, KV_LORA), lambda i, j: (0, 0, 0)),
                  pl.BlockSpec((ts, ROPE), lambda i, j: (j, 0)),
                  pl.BlockSpec((ts, ROPE), lambda i, j: (j, 0))],
        out_specs=[pl.BlockSpec((1, ts, KV_IN), lambda i, j: (i, j, 0)),
                   pl.BlockSpec((1, N_HEADS, ts, KV_IN), lambda i, j: (i, 0, j, 0)),
                   pl.BlockSpec((1, ts, D_MODEL), lambda i, j: (i, j, 0)),
                   pl.BlockSpec((1, ts, 2 * D_MODEL), lambda i, j: (i, j, 0))],
        out_shape=[jax.ShapeDtypeStruct((b, s, KV_IN), BF16),
                   jax.ShapeDtypeStruct((b, N_HEADS, s, KV_IN), BF16),
                   jax.ShapeDtypeStruct((b, s, D_MODEL), BF16),
                   jax.ShapeDtypeStruct((b, s, 2 * D_MODEL), BF16)],
        compiler_params=pltpu.CompilerParams(vmem_limit_bytes=VMEM_LIMIT),
        name="proj",
    )(x, mod, g_pre, win_r, g_q, wuq_r, g_kv, wk_abs, cos_t, sin_t)


def _attn_kernel(q_ref, k_ref, wv_ref, o_ref, s_ref, p_ref):
    k = k_ref[0]
    v = k[:, 0:KV_LORA]

    def scores(hd):
        s_ref[hd % 2] = lax.dot_general(q_ref[0, hd], k, _NT, preferred_element_type=F32)

    scores(0)
    for hd in range(N_HEADS):
        par = hd % 2
        if hd + 1 < N_HEADS:
            scores(hd + 1)
        s = s_ref[par]
        p = jnp.exp(s - jnp.max(s, axis=-1, keepdims=True))
        l = jnp.sum(p, axis=-1, keepdims=True)
        p_ref[par] = p.astype(BF16)
        o = jnp.dot(p_ref[par], v, preferred_element_type=F32) / l
        a = jnp.dot(o.astype(BF16), wv_ref[hd], preferred_element_type=F32)
        o_ref[0, :, hd * VHEAD:(hd + 1) * VHEAD] = a.astype(BF16)


def _attn_call(q, kall, wv):
    b, _, s, _ = q.shape
    lk = kall.shape[1]
    tq = min(TQ_ATTN, s)
    return pl.pallas_call(
        _attn_kernel,
        grid=(b, s // tq),
        in_specs=[pl.BlockSpec((1, N_HEADS, tq, KV_IN), lambda i, j: (i, 0, j, 0)),
                  pl.BlockSpec((1, lk, KV_IN), lambda i, j: (i, 0, 0)),
                  pl.BlockSpec((N_HEADS, KV_LORA, VHEAD), lambda i, j: (0, 0, 0))],
        out_specs=pl.BlockSpec((1, tq, D_MODEL), lambda i, j: (i, j, 0)),
        out_shape=jax.ShapeDtypeStruct((b, s, D_MODEL), BF16),
        scratch_shapes=[pltpu.VMEM((2, tq, lk), F32), pltpu.VMEM((2, tq, lk), BF16)],
        compiler_params=pltpu.CompilerParams(vmem_limit_bytes=VMEM_LIMIT),
        name="attn",
    )(q, kall, wv)


def _pool_kernel(xp_ref, w_ref, ps_ref, o_ref):
    g = pl.program_id(1)
    x = xp_ref[0].astype(F32)
    n = x.shape[0]
    t = lax.broadcasted_iota(jnp.int32, x.shape, 0)
    pad = POOL_MAX_HALF
    zeros = jnp.zeros((pad, x.shape[1]), F32)
    xe = jnp.concatenate([zeros, x, zeros], axis=0)
    ne = n + 2 * pad

    def down(a, k):
        return pltpu.roll(a, k, 0)

    def up(a, k):
        return pltpu.roll(a, ne - k, 0)

    w = down(xe, 1) + xe
    for lvl in range(1, POOL_GROUPS):
        k = 1 << (lvl - 1)
        w = jnp.where(g >= lvl, down(w, k) + up(w, k), w)
    w = w[pad:pad + n]
    half = lax.shift_left(jnp.int32(1), g)
    count = (jnp.minimum(t + half, n) - jnp.maximum(t - half, 0)).astype(F32)
    pooled = (w / count - x).astype(BF16)
    y = jnp.dot(pooled, w_ref[0], preferred_element_type=F32) * ps_ref[...]
    o_ref[0] = y.astype(BF16)


def _pool_call(pool_in, w_pool, pool_scale):
    b, s, _ = pool_in.shape
    gd = POOL_GROUP_DIM
    return pl.pallas_call(
        _pool_kernel,
        grid=(b, POOL_GROUPS),
        in_specs=[pl.BlockSpec((1, s, gd), lambda i, g: (i, 0, g)),
                  pl.BlockSpec((1, gd, gd), lambda i, g: (g, 0, 0)),
                  pl.BlockSpec((1, gd), lambda i, g: (0, g))],
        out_specs=pl.BlockSpec((1, s, gd), lambda i, g: (i, 0, g)),
        out_shape=jax.ShapeDtypeStruct((b, s, D_MODEL), BF16),
        compiler_params=pltpu.CompilerParams(vmem_limit_bytes=VMEM_LIMIT),
        name="pool",
    )(pool_in, w_pool, pool_scale)


SUB = 8


def _sort_network(n):
    pairs = []
    p = 1
    while p < PEER_TOPK:
        k = p
        while k >= 1:
            for j in range(k % p, PEER_TOPK - k, 2 * k):
                for i in range(min(k, PEER_TOPK - j - k)):
                    if (i + j) // (2 * p) == (i + j + k) // (2 * p):
                        pairs.append((i + j, i + j + k))
            k //= 2
        p *= 2
    return [(i, j) for i, j in pairs if j < n]


def _top16_sorted(vs):
    vs = list(vs)
    for i, j in _sort_network(len(vs)):
        vs[i], vs[j] = jnp.maximum(vs[i], vs[j]), jnp.minimum(vs[i], vs[j])
    vs += [jnp.full(vs[0].shape, -jnp.inf, F32)] * (PEER_TOPK - len(vs))
    shift = SUB // 2
    while shift >= 1:
        other = [pltpu.roll(v, shift, 0) for v in vs]
        vs = [jnp.maximum(vs[k], other[PEER_TOPK - 1 - k]) for k in range(PEER_TOPK)]
        d = PEER_TOPK // 2
        while d >= 1:
            for i in range(PEER_TOPK):
                if i & d == 0:
                    vs[i], vs[i + d] = jnp.maximum(vs[i], vs[i + d]), jnp.minimum(vs[i], vs[i + d])
            d //= 2
        shift //= 2
    return vs


def _peer_stats(s1, s2):
    lanes = s1.shape[1]
    n_v = N_KEYS // SUB
    rows = lax.broadcasted_iota(jnp.int32, (SUB, lanes), 0)
    v1 = [s1[SUB * k:SUB * (k + 1)] for k in range(n_v)]
    v2 = [s2[SUB * k:SUB * (k + 1)] for k in range(n_v)]
    a = _top16_sorted(v1)
    b = _top16_sorted(v2)

    def by_rank(vals):
        out = vals[0]
        for r in range(1, SUB):
            out = jnp.where(rows == r, vals[r], out)
        return out

    a_lo, a_hi, b_hi = by_rank(a[0:SUB]), by_rank(a[SUB:]), by_rank(b[SUB:])
    pieces = [a_lo + b[0], a_hi + b[0]] + [a_lo + b[c] for c in range(1, SUB)] + [b_hi + a[0]]
    thr = _top16_sorted(pieces)[PEER_TOPK - 1]
    top = a[0] + b[0]
    sel = [p >= thr for p in pieces]
    z = sum(jnp.where(m, jnp.exp(p - top), 0.0) for m, p in zip(sel, pieces))
    inv_z = 1.0 / jnp.broadcast_to(jnp.sum(z, axis=0, keepdims=True), (SUB, lanes))
    one = [jnp.where(m, 1.0, 0.0) for m in sel]
    n_hi = one[1]
    n_lo = one[0] + sum(one[2:2 + SUB - 1])
    extra = jnp.broadcast_to(jnp.sum(one[-1], axis=0, keepdims=True), (SUB, lanes))
    n_lo = n_lo + jnp.where(rows == 0, extra, 0.0)
    n_of = [jnp.broadcast_to((n_lo if r < SUB else n_hi)[r % SUB:r % SUB + 1], (SUB, lanes))
            for r in range(PEER_TOPK)]
    cnt, e1, rank2, e2 = [], [], [], []
    for k in range(n_v):
        c = jnp.zeros((SUB, lanes), F32)
        rk = jnp.zeros((SUB, lanes), F32)
        for r in range(PEER_TOPK):
            c = jnp.where(v1[k] == a[r], n_of[r], c)
            rk = jnp.where(v2[k] < b[r], float(r + 1), rk)
        cnt.append(c)
        rank2.append(rk)
        e1.append(jnp.exp(v1[k] - a[0]) * inv_z)
        e2.append(jnp.exp(v2[k] - b[0]))
    cat = lambda vs: jnp.concatenate(vs, axis=0)
    return cat(cnt), cat(e1), cat(rank2), cat(e2)


def _merge_kernel(x_ref, attn_ref, yp_ref, gates_ref, mod_ref, wout_ref, gpost_ref, gffn_ref, wq_ref, keys_ref,
                  xmid_ref, hf_ref, rk_ref, e2_ref, cnt_ref, e1_ref):
    m = mod_ref[0]
    ga = gates_ref[0, :, 0:D_MODEL].astype(F32)
    gb = gates_ref[0, :, D_MODEL:2 * D_MODEL].astype(F32)
    merged = ga * attn_ref[0].astype(F32) + gb * yp_ref[0].astype(F32)
    y = jnp.dot(merged.astype(BF16), wout_ref[...], preferred_element_type=F32)
    x_mid = x_ref[0] + m[2:3] * (_rms(y) * gpost_ref[...])
    xmid_ref[0] = x_mid
    hf = (_rms(x_mid) * gffn_ref[...]) * (1.0 + m[4:5]) + m[3:4]
    hfb = hf.astype(BF16)
    hf_ref[0] = hfb
    q = jnp.dot(hfb, wq_ref[...], preferred_element_type=F32).astype(BF16)
    k1 = keys_ref[0]
    k2 = keys_ref[1]
    ts = q.shape[0]
    for hd in range(PEER_HEADS):
        base = hd * 2 * PEER_HALF
        s1 = lax.dot_general(k1, q[:, base:base + PEER_HALF], _NT, preferred_element_type=F32)
        s2 = lax.dot_general(k2, q[:, base + PEER_HALF:base + 2 * PEER_HALF], _NT, preferred_element_type=F32)
        for lt in range(ts // LANE):
            sl = slice(lt * LANE, (lt + 1) * LANE)
            blk, off = divmod(lt * LANE, PEER_LT)
            dst = slice(off, off + LANE)
            cnt, e1, rank2, e2 = _peer_stats(s1[:, sl], s2[:, sl])
            rk_ref[blk, hd, :, dst] = rank2.astype(BF16)
            e2_ref[blk, hd, :, dst] = e2.astype(BF16)
            cnt_ref[blk, hd, :, dst] = cnt
            e1_ref[blk, hd, :, dst] = e1


def _merge_call(x, attn, ypool, gates, mod, w_out, g_post, g_ffn, wq, keys):
    b, s, _ = x.shape
    ts = min(TS_MERGE, s)
    nt = s // ts
    n_lt = b * s // PEER_LT
    const2 = lambda i, j: (0, 0)
    tok3 = lambda i, j: (i, j, 0)
    lane_t = lambda i, j: (i * nt + j, 0, 0, 0)
    return pl.pallas_call(
        _merge_kernel,
        grid=(b, nt),
        in_specs=[pl.BlockSpec((1, ts, D_MODEL), tok3),
                  pl.BlockSpec((1, ts, D_MODEL), tok3),
                  pl.BlockSpec((1, ts, D_MODEL), tok3),
                  pl.BlockSpec((1, ts, 2 * D_MODEL), tok3),
                  pl.BlockSpec((1, 6, D_MODEL), lambda i, j: (i, 0, 0)),
                  pl.BlockSpec((D_MODEL, D_MODEL), const2),
                  pl.BlockSpec((1, D_MODEL), const2),
                  pl.BlockSpec((1, D_MODEL), const2),
                  pl.BlockSpec((D_MODEL, PEER_HEADS * 2 * PEER_HALF), const2),
                  pl.BlockSpec((2, N_KEYS, PEER_HALF), lambda i, j: (0, 0, 0))],
        out_specs=[pl.BlockSpec((1, ts, D_MODEL), tok3),
                   pl.BlockSpec((1, ts, D_MODEL), tok3),
                   pl.BlockSpec((ts // PEER_LT, PEER_HEADS, N_KEYS, PEER_LT), lane_t),
                   pl.BlockSpec((ts // PEER_LT, PEER_HEADS, N_KEYS, PEER_LT), lane_t),
                   pl.BlockSpec((ts // PEER_LT, PEER_HEADS, N_KEYS, PEER_LT), lane_t),
                   pl.BlockSpec((ts // PEER_LT, PEER_HEADS, N_KEYS, PEER_LT), lane_t)],
        out_shape=[jax.ShapeDtypeStruct((b, s, D_MODEL), F32),
                   jax.ShapeDtypeStruct((b, s, D_MODEL), BF16),
                   jax.ShapeDtypeStruct((n_lt, PEER_HEADS, N_KEYS, PEER_LT), BF16),
                   jax.ShapeDtypeStruct((n_lt, PEER_HEADS, N_KEYS, PEER_LT), BF16),
                   jax.ShapeDtypeStruct((n_lt, PEER_HEADS, N_KEYS, PEER_LT), F32),
                   jax.ShapeDtypeStruct((n_lt, PEER_HEADS, N_KEYS, PEER_LT), F32)],
        compiler_params=pltpu.CompilerParams(vmem_limit_bytes=VMEM_LIMIT),
        name="merge",
    )(x, attn, ypool, gates, mod, w_out, g_post, g_ffn, wq, keys)


def _peer_kernel(hf_ref, u_ref, vt_ref, rk_ref, e2_ref, cnt_ref, e1_ref, xmid_ref, mod_ref, gpost_ref,
                 o_ref, acc_ref, act_ref, a_ref):
    e = pl.program_id(2)
    n_lt = hf_ref.shape[1] // PEER_LT

    @pl.when(e == 0)
    def _():
        acc_ref[...] = jnp.zeros_like(acc_ref)

    for lt in range(n_lt):
        hf_blk = hf_ref[0, lt * PEER_LT:(lt + 1) * PEER_LT, :]
        act_ref[lt] = lax.dot_general(u_ref[...], hf_blk, _NT, preferred_element_type=F32)

    for lt in range(n_lt):
        cnt_hd = [cnt_ref[lt, hd].astype(BF16) for hd in range(PEER_HEADS)]
        e1_hd = [e1_ref[lt, hd].astype(BF16) for hd in range(PEER_HEADS)]
        for i in range(EI_PEER):
            w = jnp.zeros((N_KEYS, PEER_LT), BF16)
            for hd in range(PEER_HEADS):
                e2 = e2_ref[lt, hd]
                hit = rk_ref[lt, hd] < cnt_hd[hd][i:i + 1]
                w = w + jnp.where(hit, e2, jnp.zeros_like(e2)) * e1_hd[hd][i:i + 1]
            rows = slice(i * N_KEYS, (i + 1) * N_KEYS)
            a = act_ref[lt, rows, :]
            gelu = 0.5 * a * (1.0 + lax.erf(a * 0.7071067811865476))
            a_ref[lt, rows, :] = w * gelu.astype(BF16)

    for lt in range(n_lt):
        acc_ref[lt] += jnp.dot(vt_ref[...], a_ref[lt], preferred_element_type=F32)

    @pl.when(e == pl.num_programs(2) - 1)
    def _():
        m = mod_ref[0]
        for lt in range(n_lt):
            tok = slice(lt * PEER_LT, (lt + 1) * PEER_LT)
            y = acc_ref[lt].T
            o_ref[0, tok, :] = xmid_ref[0, tok, :] + m[5:6] * (_rms(y) * gpost_ref[...])


def _peer_call(hf, u_b, vt_b, s2t, e2t, cutt, e1t, x_mid, mod, g_post):
    b, s, _ = hf.shape
    tt = min(TT_PEER, s)
    nt = s // tt
    et = EI_PEER * N_KEYS
    ne = N_KEYS // EI_PEER
    tok3 = lambda i, j, e: (i, j, 0)
    lane_t = lambda i, j, e: (i * nt + j, 0, 0, 0)
    return pl.pallas_call(
        _peer_kernel,
        grid=(b, nt, ne),
        in_specs=[pl.BlockSpec((1, tt, D_MODEL), tok3),
                  pl.BlockSpec((et, D_MODEL), lambda i, j, e: (e, 0)),
                  pl.BlockSpec((D_MODEL, et), lambda i, j, e: (0, e)),
                  pl.BlockSpec((tt // PEER_LT, PEER_HEADS, N_KEYS, PEER_LT), lane_t),
                  pl.BlockSpec((tt // PEER_LT, PEER_HEADS, N_KEYS, PEER_LT), lane_t),
                  pl.BlockSpec((tt // PEER_LT, PEER_HEADS, EI_PEER, PEER_LT), lambda i, j, e: (i * nt + j, 0, e, 0)),
                  pl.BlockSpec((tt // PEER_LT, PEER_HEADS, EI_PEER, PEER_LT), lambda i, j, e: (i * nt + j, 0, e, 0)),
                  pl.BlockSpec((1, tt, D_MODEL), tok3),
                  pl.BlockSpec((1, 6, D_MODEL), lambda i, j, e: (i, 0, 0)),
                  pl.BlockSpec((1, D_MODEL), lambda i, j, e: (0, 0))],
        out_specs=pl.BlockSpec((1, tt, D_MODEL), tok3),
        out_shape=jax.ShapeDtypeStruct((b, s, D_MODEL), F32),
        scratch_shapes=[pltpu.VMEM((tt // PEER_LT, D_MODEL, PEER_LT), F32),
                        pltpu.VMEM((tt // PEER_LT, et, PEER_LT), F32),
                        pltpu.VMEM((tt // PEER_LT, et, PEER_LT), BF16)],
        compiler_params=pltpu.CompilerParams(
            vmem_limit_bytes=VMEM_LIMIT,
            dimension_semantics=("arbitrary", "arbitrary", "arbitrary")),
        name="peer",
    )(hf, u_b, vt_b, s2t, e2t, cutt, e1t, x_mid, mod, g_post)


def _rope_tables(seq):
    rows = seq // GRID_W
    row = jnp.broadcast_to(jnp.arange(rows, dtype=F32)[:, None], (rows, GRID_W)).reshape(-1)
    col = jnp.broadcast_to(jnp.arange(GRID_W, dtype=F32)[None, :], (rows, GRID_W)).reshape(-1)
    n_freq = ROPE // 4
    inv_freq = ROPE_BASE ** (-jnp.arange(n_freq, dtype=F32) / n_freq)
    ar = row[:, None] * inv_freq
    ac = col[:, None] * inv_freq
    cos = jnp.concatenate([jnp.cos(ar), jnp.cos(ar), jnp.cos(ac), jnp.cos(ac)], axis=1)
    sin = jnp.concatenate([-jnp.sin(ar), jnp.sin(ar), -jnp.sin(ac), jnp.sin(ac)], axis=1)
    return cos, sin


def _swap_halves(w):
    q = ROPE // 4
    return jnp.concatenate([w[..., q:2 * q], w[..., 0:q], w[..., 3 * q:4 * q], w[..., 2 * q:3 * q]], axis=-1)


def kernel(x, c, ctx, c_ctx, w_ada, b_ada, g_mix_pre, g_mix_post, g_ffn_pre, g_ffn_post, w_in, g_q_lora, w_uq,
           g_kv_lora, w_ukv, w_pool, pool_scale, w_out, peer_wq, peer_keys, peer_u, peer_v):
    assert w_ada.shape[0] == 1, "single-layer block"
    b, s, _ = x.shape
    layer = 0

    w_in_l = w_in[layer]
    k_rope_w = w_in_l[:, KV_LORA:KV_IN]
    win_r = jnp.concatenate([w_in_l[:, 0:KV_IN], _swap_halves(k_rope_w), w_in_l[:, KV_IN:]], axis=1).astype(BF16)
    wuq_h = w_uq[layer].reshape(Q_LORA, N_HEADS, QK_HEAD)
    wuq_r = jnp.concatenate([wuq_h, _swap_halves(wuq_h[..., NOPE:])], axis=-1).reshape(Q_LORA, N_HEADS * QH_W)
    wuq_r = wuq_r.astype(BF16)
    wukv_h = w_ukv[layer].reshape(KV_LORA, N_HEADS, NOPE + VHEAD)
    wk_abs = jnp.transpose(wukv_h[..., 0:NOPE], (1, 2, 0)).astype(BF16)
    wv_up = jnp.transpose(wukv_h[..., NOPE:], (1, 0, 2)).astype(BF16)
    cos_t, sin_t = _rope_tables(s)
    u_b = peer_u[layer].astype(BF16)
    vt_b = peer_v[layer].T.astype(BF16)
    row = lambda v: v[layer][None, :]

    n_rows = -(-(b + 1) // 8) * 8
    cc = jnp.concatenate([c, c_ctx[None, :], jnp.zeros((n_rows - b - 1, D_MODEL), F32)], axis=0)
    mod_all = _mod_call(cc, w_ada[layer], b_ada[layer][None, :])
    mod = mod_all[:b].reshape(b, 6, D_MODEL)
    mod_c = mod_all[b].reshape(6, D_MODEL)

    k_ctx = _ctx_call(ctx, mod_c, row(g_mix_pre), win_r[:, 0:256], row(g_kv_lora))
    k_lat, q, pool_in, gates = _proj_call(x, mod, row(g_mix_pre), win_r, row(g_q_lora), wuq_r, row(g_kv_lora),
                                          wk_abs, cos_t, sin_t)
    k_all = jnp.concatenate([k_ctx, k_lat], axis=1)
    attn = _attn_call(q, k_all, wv_up)
    ypool = _pool_call(pool_in, w_pool[layer].astype(BF16), row(pool_scale))
    x_mid, hf, s2t, e2t, cutt, e1t = _merge_call(
        x, attn, ypool, gates, mod, w_out[layer].astype(BF16), row(g_mix_post), row(g_ffn_pre),
        peer_wq[layer].astype(BF16), peer_keys[layer].astype(BF16))
    return _peer_call(hf, u_b, vt_b, s2t, e2t, cutt, e1t, x_mid, mod, row(g_ffn_post))
```

```python
import functools

import jax
import jax.numpy as jnp
from jax import lax
from jax.experimental import pallas as pl
from jax.experimental.pallas import tpu as pltpu

F32 = jnp.float32
BF16 = jnp.bfloat16

D_MODEL = 1024
GRID_W = 64
N_HEADS = 8
NOPE = 128
ROPE = 64
VHEAD = 128
QK_HEAD = NOPE + ROPE
Q_LORA = 256
KV_LORA = 128
ROPE_BASE = 10000.0
ATTN_SCALE = QK_HEAD ** -0.5
Q_SCALE = ATTN_SCALE * 1.4426950408889634
POOL_GROUPS = 4
POOL_GROUP_DIM = D_MODEL // POOL_GROUPS
POOL_MAX_HALF = 1 << (POOL_GROUPS - 1)
KV_IN = KV_LORA + ROPE
PEER_HEADS = 8
N_KEYS = 128
PEER_HALF = 128
PEER_TOPK = 16
EPS = 1e-6

WIN_SMALL = 512
WIN_POOL = WIN_SMALL
WIN_GATE = WIN_POOL + D_MODEL
WIN_TOTAL = WIN_GATE + 2 * D_MODEL
QH_W = 256

LANE = 128
PEER_LT = 256
VMEM_LIMIT = 56 * 1024 * 1024

TS_PROJ = 512
TQ_ATTN = 256
PV_ROWS = 128
TS_MERGE = 256
TT_PEER = 1024
EI_PEER = 8

_NT = (((1,), (1,)), ((), ()))


def _rms(x):
    return x * lax.rsqrt(jnp.mean(x * x, axis=-1, keepdims=True) + EPS)


def _mod_kernel(cc_ref, w_ref, b_ref, o_ref):
    cc = cc_ref[...]
    a = cc * jax.nn.sigmoid(cc)
    o_ref[...] = jnp.dot(a, w_ref[...], preferred_element_type=F32) + b_ref[...]


def _mod_call(cc, w_ada, b_ada):
    rows = cc.shape[0]
    n = w_ada.shape[1]
    bn = 1024
    return pl.pallas_call(
        _mod_kernel,
        grid=(n // bn,),
        in_specs=[pl.BlockSpec((rows, D_MODEL), lambda j: (0, 0)),
                  pl.BlockSpec((D_MODEL, bn), lambda j: (0, j)),
                  pl.BlockSpec((1, bn), lambda j: (0, j))],
        out_specs=pl.BlockSpec((rows, bn), lambda j: (0, j)),
        out_shape=jax.ShapeDtypeStruct((rows, n), F32),
        compiler_params=pltpu.CompilerParams(vmem_limit_bytes=VMEM_LIMIT),
        name="mod",
    )(cc, w_ada, b_ada)


def _ctx_kernel(ctx_ref, mod_ref, gpre_ref, win_ref, gkv_ref, o_ref):
    x = ctx_ref[0]
    m = mod_ref[...]
    h = _rms(x) * gpre_ref[...]
    h = h * (1.0 + m[1:2]) + m[0:1]
    p = jnp.dot(h.astype(BF16), win_ref[...], preferred_element_type=F32)
    ckv_n = _rms(p[:, :KV_LORA]) * gkv_ref[...]
    o_ref[0, :, 0:KV_LORA] = ckv_n.astype(BF16)
    o_ref[0, :, KV_LORA:KV_IN] = p[:, KV_LORA:KV_IN].astype(BF16)


def _ctx_call(ctx, mod_c, g_pre, win_r, g_kv):
    b, lc, _ = ctx.shape
    return pl.pallas_call(
        _ctx_kernel,
        grid=(b,),
        in_specs=[pl.BlockSpec((1, lc, D_MODEL), lambda i: (i, 0, 0)),
                  pl.BlockSpec((6, D_MODEL), lambda i: (0, 0)),
                  pl.BlockSpec((1, D_MODEL), lambda i: (0, 0)),
                  pl.BlockSpec((D_MODEL, 256), lambda i: (0, 0)),
                  pl.BlockSpec((1, KV_LORA), lambda i: (0, 0))],
        out_specs=pl.BlockSpec((1, lc, KV_IN), lambda i: (i, 0, 0)),
        out_shape=jax.ShapeDtypeStruct((b, lc, KV_IN), BF16),
        compiler_params=pltpu.CompilerParams(vmem_limit_bytes=VMEM_LIMIT),
        name="ctx_kv",
    )(ctx, mod_c, g_pre, win_r, g_kv)


def _proj_kernel(x_ref, mod_ref, gpre_ref, win_ref, gq_ref, wuq_ref, gkv_ref, wk_ref, cos_ref, sin_ref,
                 kcat_ref, q_ref, pool_ref, gates_ref):
    x = x_ref[0]
    m = mod_ref[0]
    h = _rms(x) * gpre_ref[...]
    hb = (h * (1.0 + m[1:2]) + m[0:1]).astype(BF16)
    cos = cos_ref[...]
    sin = sin_ref[...]

    p0 = jnp.dot(hb, win_ref[:, 0:WIN_SMALL], preferred_element_type=F32)
    ckv_n = _rms(p0[:, 0:KV_LORA]) * gkv_ref[...]
    k_rot = p0[:, 128:192] * cos + p0[:, 192:256] * sin
    kcat_ref[0, :, 0:KV_LORA] = ckv_n.astype(BF16)
    kcat_ref[0, :, KV_LORA:KV_IN] = k_rot.astype(BF16)

    cqn = (_rms(p0[:, 256:512]) * gq_ref[...]).astype(BF16)

    pool_ref[0] = jnp.dot(hb, win_ref[:, WIN_POOL:WIN_GATE], preferred_element_type=F32).astype(BF16)
    qh = [jnp.dot(cqn, wuq_ref[:, hd * QH_W:(hd + 1) * QH_W], preferred_element_type=F32)
          for hd in range(N_HEADS)]
    for half in range(2):
        lo = WIN_GATE + half * D_MODEL
        g = jnp.dot(hb, win_ref[:, lo:lo + D_MODEL], preferred_element_type=F32)
        gates_ref[0, :, half * D_MODEL:(half + 1) * D_MODEL] = jax.nn.sigmoid(g).astype(BF16)
    for hd in range(N_HEADS):
        q_abs = jnp.dot(qh[hd][:, 0:NOPE].astype(BF16), wk_ref[hd], preferred_element_type=F32)
        q_rot = qh[hd][:, 128:192] * cos + qh[hd][:, 192:256] * sin
        q_ref[0, hd, :, 0:KV_LORA] = (q_abs * Q_SCALE).astype(BF16)
        q_ref[0, hd, :, KV_LORA:KV_IN] = (q_rot * Q_SCALE).astype(BF16)


def _proj_call(x, mod, g_pre, win_r, g_q, wuq_r, g_kv, wk_abs, cos_t, sin_t):
    b, s, _ = x.shape
    ts = min(TS_PROJ, s)
    const2 = lambda i, j: (0, 0)
    return pl.pallas_call(
        _proj_kernel,
        grid=(b, s // ts),
        in_specs=[pl.BlockSpec((1, ts, D_MODEL), lambda i, j: (i, j, 0)),
                  pl.BlockSpec((1, 6, D_MODEL), lambda i, j: (i, 0, 0)),
                  pl.BlockSpec((1, D_MODEL), const2),
                  pl.BlockSpec((D_MODEL, WIN_TOTAL), const2),
                  pl.BlockSpec((1, Q_LORA), const2),
                  pl.BlockSpec((Q_LORA, N_HEADS * QH_W), const2),
                  pl.BlockSpec((1, KV_LORA), const2),
                  pl.BlockSpec((N_HEADS, NOPE, KV_LORA), lambda i, j: (0, 0, 0)),
                  pl.BlockSpec((ts, ROPE), lambda i, j: (j, 0)),
                  pl.BlockSpec((ts, ROPE), lambda i, j: (j, 0))],
        out_specs=[pl.BlockSpec((1, ts, KV_IN), lambda i, j: (i, j, 0)),
                   pl.BlockSpec((1, N_HEADS, ts, KV_IN), lambda i, j: (i, 0, j, 0)),
                   pl.BlockSpec((1, ts, D_MODEL), lambda i, j: (i, j, 0)),
                   pl.BlockSpec((1, ts, 2 * D_MODEL), lambda i, j: (i, j, 0))],
        out_shape=[jax.ShapeDtypeStruct((b, s, KV_IN), BF16),
                   jax.ShapeDtypeStruct((b, N_HEADS, s, KV_IN), BF16),
                   jax.ShapeDtypeStruct((b, s, D_MODEL), BF16),
                   jax.ShapeDtypeStruct((b, s, 2 * D_MODEL), BF16)],
        compiler_params=pltpu.CompilerParams(vmem_limit_bytes=VMEM_LIMIT),
        name="proj",
    )(x, mod, g_pre, win_r, g_q, wuq_r, g_kv, wk_abs, cos_t, sin_t)


def _attn_kernel(q_ref, k_ref, wv_ref, o_ref, s_ref, p_ref):
    k = k_ref[0]
    v = k[:, 0:KV_LORA]

    def scores(hd):
        s_ref[hd % 2] = lax.dot_general(q_ref[0, hd], k, _NT, preferred_element_type=F32)

    def project(hd, o):
        a = jnp.dot(o, wv_ref[hd], preferred_element_type=F32)
        o_ref[0, :, hd * VHEAD:(hd + 1) * VHEAD] = a.astype(BF16)

    scores(0)
    pending = None
    for hd in range(N_HEADS):
        par = hd % 2
        if hd + 1 < N_HEADS:
            scores(hd + 1)
        s = s_ref[par]
        p = jnp.exp2(s - jnp.max(s, axis=-1, keepdims=True))
        l = jnp.sum(p, axis=-1, keepdims=True)
        p_ref[par] = p.astype(BF16)
        o = jnp.concatenate([jnp.dot(p_ref[par, r0:r0 + PV_ROWS, :], v, preferred_element_type=F32)
                             for r0 in range(0, s.shape[0], PV_ROWS)], axis=0) / l
        if pending is not None:
            project(*pending)
        pending = (hd, o.astype(BF16))
    project(*pending)


def _attn_call(q, kall, wv):
    b, _, s, _ = q.shape
    lk = kall.shape[1]
    tq = min(TQ_ATTN, s)
    return pl.pallas_call(
        _attn_kernel,
        grid=(b, s // tq),
        in_specs=[pl.BlockSpec((1, N_HEADS, tq, KV_IN), lambda i, j: (i, 0, j, 0)),
                  pl.BlockSpec((1, lk, KV_IN), lambda i, j: (i, 0, 0)),
                  pl.BlockSpec((N_HEADS, KV_LORA, VHEAD), lambda i, j: (0, 0, 0))],
        out_specs=pl.BlockSpec((1, tq, D_MODEL), lambda i, j: (i, j, 0)),
        out_shape=jax.ShapeDtypeStruct((b, s, D_MODEL), BF16),
        scratch_shapes=[pltpu.VMEM((2, tq, lk), F32), pltpu.VMEM((2, tq, lk), BF16)],
        compiler_params=pltpu.CompilerParams(vmem_limit_bytes=VMEM_LIMIT),
        name="attn",
    )(q, kall, wv)


def _pool_kernel(xp_ref, w_ref, ps_ref, o_ref):
    g = pl.program_id(1)
    x = xp_ref[0].astype(F32)
    n = x.shape[0]
    t = lax.broadcasted_iota(jnp.int32, x.shape, 0)
    pad = POOL_MAX_HALF
    zeros = jnp.zeros((pad, x.shape[1]), F32)
    xe = jnp.concatenate([zeros, x, zeros], axis=0)
    ne = n + 2 * pad

    def down(a, k):
        return pltpu.roll(a, k, 0)

    def up(a, k):
        return pltpu.roll(a, ne - k, 0)

    w = down(xe, 1) + xe
    for lvl in range(1, POOL_GROUPS):
        k = 1 << (lvl - 1)
        w = jnp.where(g >= lvl, down(w, k) + up(w, k), w)
    w = w[pad:pad + n]
    half = lax.shift_left(jnp.int32(1), g)
    count = (jnp.minimum(t + half, n) - jnp.maximum(t - half, 0)).astype(F32)
    pooled = (w / count - x).astype(BF16)
    y = jnp.dot(pooled, w_ref[0], preferred_element_type=F32) * ps_ref[...]
    o_ref[0] = y.astype(BF16)


def _pool_call(pool_in, w_pool, pool_scale):
    b, s, _ = pool_in.shape
    gd = POOL_GROUP_DIM
    return pl.pallas_call(
        _pool_kernel,
        grid=(b, POOL_GROUPS),
        in_specs=[pl.BlockSpec((1, s, gd), lambda i, g: (i, 0, g)),
                  pl.BlockSpec((1, gd, gd), lambda i, g: (g, 0, 0)),
                  pl.BlockSpec((1, gd), lambda i, g: (0, g))],
        out_specs=pl.BlockSpec((1, s, gd), lambda i, g: (i, 0, g)),
        out_shape=jax.ShapeDtypeStruct((b, s, D_MODEL), BF16),
        compiler_params=pltpu.CompilerParams(vmem_limit_bytes=VMEM_LIMIT),
        name="pool",
    )(pool_in, w_pool, pool_scale)


SUB = 8


def _sort_network(n):
    pairs = []
    p = 1
    while p < PEER_TOPK:
        k = p
        while k >= 1:
            for j in range(k % p, PEER_TOPK - k, 2 * k):
                for i in range(min(k, PEER_TOPK - j - k)):
                    if (i + j) // (2 * p) == (i + j + k) // (2 * p):
                        pairs.append((i + j, i + j + k))
            k //= 2
        p *= 2
    return [(i, j) for i, j in pairs if j < n]


def _top16_sorted(vs, sort_last=True):
    vs = list(vs)
    for i, j in _sort_network(len(vs)):
        vs[i], vs[j] = jnp.maximum(vs[i], vs[j]), jnp.minimum(vs[i], vs[j])
    vs += [jnp.full(vs[0].shape, -jnp.inf, F32)] * (PEER_TOPK - len(vs))
    shift = SUB // 2
    while shift >= 1:
        other = [pltpu.roll(v, shift, 0) for v in vs]
        vs = [jnp.maximum(vs[k], other[PEER_TOPK - 1 - k]) for k in range(PEER_TOPK)]
        d = PEER_TOPK // 2 if (sort_last or shift > 1) else 0
        while d >= 1:
            for i in range(PEER_TOPK):
                if i & d == 0:
                    vs[i], vs[i + d] = jnp.maximum(vs[i], vs[i + d]), jnp.minimum(vs[i], vs[i + d])
            d //= 2
        shift //= 2
    return vs


def _peer_stats(s1, s2):
    lanes = s1.shape[1]
    n_v = N_KEYS // SUB
    rows = lax.broadcasted_iota(jnp.int32, (SUB, lanes), 0)
    v1 = [s1[SUB * k:SUB * (k + 1)] for k in range(n_v)]
    v2 = [s2[SUB * k:SUB * (k + 1)] for k in range(n_v)]
    a = _top16_sorted(v1)
    b = _top16_sorted(v2)

    def by_rank(vals):
        out = vals[0]
        for r in range(1, SUB):
            out = jnp.where(rows == r, vals[r], out)
        return out

    a_lo, a_hi, b_hi = by_rank(a[0:SUB]), by_rank(a[SUB:]), by_rank(b[SUB:])
    pieces = [a_lo + b[0], a_hi + b[0]] + [a_lo + b[c] for c in range(1, SUB)] + [b_hi + a[0]]
    thr = functools.reduce(jnp.minimum, _top16_sorted(pieces, sort_last=False))
    top = a[0] + b[0]
    sel = [p >= thr for p in pieces]
    z = sum(jnp.where(m, jnp.exp(p - top), 0.0) for m, p in zip(sel, pieces))
    inv_z = 1.0 / jnp.broadcast_to(jnp.sum(z, axis=0, keepdims=True), (SUB, lanes))
    one = [jnp.where(m, 1.0, 0.0) for m in sel]
    n_hi = one[1]
    n_lo = one[0] + sum(one[2:2 + SUB - 1])
    extra = jnp.broadcast_to(jnp.sum(one[-1], axis=0, keepdims=True), (SUB, lanes))
    n_lo = n_lo + jnp.where(rows == 0, extra, 0.0)
    n_of = [jnp.broadcast_to((n_lo if r < SUB else n_hi)[r % SUB:r % SUB + 1], (SUB, lanes))
            for r in range(PEER_TOPK)]
    cnt, e1, rank2, e2 = [], [], [], []
    for k in range(n_v):
        c = jnp.zeros((SUB, lanes), F32)
        rk = jnp.zeros((SUB, lanes), F32)
        for r in range(PEER_TOPK):
            c = jnp.where(v1[k] == a[r], n_of[r], c)
            rk = jnp.where(v2[k] < b[r], float(r + 1), rk)
        cnt.append(c)
        rank2.append(rk)
        e1.append(jnp.exp(v1[k] - a[0]) * inv_z)
        e2.append(jnp.exp(v2[k] - b[0]))
    cat = lambda vs: jnp.concatenate(vs, axis=0)
    return cat(cnt), cat(e1), cat(rank2), cat(e2)


def _merge_kernel(x_ref, attn_ref, yp_ref, gates_ref, mod_ref, wout_ref, gpost_ref, gffn_ref, wq_ref, keys_ref,
                  xmid_ref, hf_ref, rk_ref, e2_ref, cnt_ref, e1_ref):
    m = mod_ref[0]
    ga = gates_ref[0, :, 0:D_MODEL].astype(F32)
    gb = gates_ref[0, :, D_MODEL:2 * D_MODEL].astype(F32)
    merged = ga * attn_ref[0].astype(F32) + gb * yp_ref[0].astype(F32)
    y = jnp.dot(merged.astype(BF16), wout_ref[...], preferred_element_type=F32)
    x_mid = x_ref[0] + m[2:3] * (_rms(y) * gpost_ref[...])
    xmid_ref[0] = x_mid
    hf = (_rms(x_mid) * gffn_ref[...]) * (1.0 + m[4:5]) + m[3:4]
    hfb = hf.astype(BF16)
    hf_ref[0] = hfb
    q = jnp.dot(hfb, wq_ref[...], preferred_element_type=F32).astype(BF16)
    k1 = keys_ref[0]
    k2 = keys_ref[1]
    ts = q.shape[0]
    for hd in range(PEER_HEADS):
        base = hd * 2 * PEER_HALF
        s1 = lax.dot_general(k1, q[:, base:base + PEER_HALF], _NT, preferred_element_type=F32)
        s2 = lax.dot_general(k2, q[:, base + PEER_HALF:base + 2 * PEER_HALF], _NT, preferred_element_type=F32)
        for lt in range(ts // LANE):
            sl = slice(lt * LANE, (lt + 1) * LANE)
            blk, off = divmod(lt * LANE, PEER_LT)
            dst = slice(off, off + LANE)
            cnt, e1, rank2, e2 = _peer_stats(s1[:, sl], s2[:, sl])
            rk_ref[blk, hd, :, dst] = rank2.astype(BF16)
            e2_ref[blk, hd, :, dst] = e2.astype(BF16)
            cnt_ref[blk, hd, :, dst] = cnt
            e1_ref[blk, hd, :, dst] = e1


def _merge_call(x, attn, ypool, gates, mod, w_out, g_post, g_ffn, wq, keys):
    b, s, _ = x.shape
    ts = min(TS_MERGE, s)
    nt = s // ts
    n_lt = b * s // PEER_LT
    const2 = lambda i, j: (0, 0)
    tok3 = lambda i, j: (i, j, 0)
    lane_t = lambda i, j: (i * nt + j, 0, 0, 0)
    return pl.pallas_call(
        _merge_kernel,
        grid=(b, nt),
        in_specs=[pl.BlockSpec((1, ts, D_MODEL), tok3),
                  pl.BlockSpec((1, ts, D_MODEL), tok3),
                  pl.BlockSpec((1, ts, D_MODEL), tok3),
                  pl.BlockSpec((1, ts, 2 * D_MODEL), tok3),
                  pl.BlockSpec((1, 6, D_MODEL), lambda i, j: (i, 0, 0)),
                  pl.BlockSpec((D_MODEL, D_MODEL), const2),
                  pl.BlockSpec((1, D_MODEL), const2),
                  pl.BlockSpec((1, D_MODEL), const2),
                  pl.BlockSpec((D_MODEL, PEER_HEADS * 2 * PEER_HALF), const2),
                  pl.BlockSpec((2, N_KEYS, PEER_HALF), lambda i, j: (0, 0, 0))],
        out_specs=[pl.BlockSpec((1, ts, D_MODEL), tok3),
                   pl.BlockSpec((1, ts, D_MODEL), tok3),
                   pl.BlockSpec((ts // PEER_LT, PEER_HEADS, N_KEYS, PEER_LT), lane_t),
                   pl.BlockSpec((ts // PEER_LT, PEER_HEADS, N_KEYS, PEER_LT), lane_t),
                   pl.BlockSpec((ts // PEER_LT, PEER_HEADS, N_KEYS, PEER_LT), lane_t),
                   pl.BlockSpec((ts // PEER_LT, PEER_HEADS, N_KEYS, PEER_LT), lane_t)],
        out_shape=[jax.ShapeDtypeStruct((b, s, D_MODEL), F32),
                   jax.ShapeDtypeStruct((b, s, D_MODEL), BF16),
                   jax.ShapeDtypeStruct((n_lt, PEER_HEADS, N_KEYS, PEER_LT), BF16),
                   jax.ShapeDtypeStruct((n_lt, PEER_HEADS, N_KEYS, PEER_LT), BF16),
                   jax.ShapeDtypeStruct((n_lt, PEER_HEADS, N_KEYS, PEER_LT), F32),
                   jax.ShapeDtypeStruct((n_lt, PEER_HEADS, N_KEYS, PEER_LT), F32)],
        compiler_params=pltpu.CompilerParams(vmem_limit_bytes=VMEM_LIMIT),
        name="merge",
    )(x, attn, ypool, gates, mod, w_out, g_post, g_ffn, wq, keys)


def _peer_kernel(hf_ref, u_ref, vt_ref, rk_ref, e2_ref, cnt_ref, e1_ref, xmid_ref, mod_ref, gpost_ref,
                 o_ref, acc_ref, act_ref, a_ref):
    e = pl.program_id(2)
    n_lt = hf_ref.shape[1] // PEER_LT

    @pl.when(e == 0)
    def _():
        acc_ref[...] = jnp.zeros_like(acc_ref)

    for lt in range(n_lt):
        hf_blk = hf_ref[0, lt * PEER_LT:(lt + 1) * PEER_LT, :]
        act_ref[lt] = lax.dot_general(u_ref[...], hf_blk, _NT, preferred_element_type=F32)

    for lt in range(n_lt):
        cnt_hd = [cnt_ref[lt, hd].astype(BF16) for hd in range(PEER_HEADS)]
        e1_hd = [e1_ref[lt, hd].astype(BF16) for hd in range(PEER_HEADS)]
        for i in range(EI_PEER):
            w = jnp.zeros((N_KEYS, PEER_LT), BF16)
            for hd in range(PEER_HEADS):
                e2 = e2_ref[lt, hd]
                hit = rk_ref[lt, hd] < cnt_hd[hd][i:i + 1]
                w = w + jnp.where(hit, e2, jnp.zeros_like(e2)) * e1_hd[hd][i:i + 1]
            rows = slice(i * N_KEYS, (i + 1) * N_KEYS)
            a = act_ref[lt, rows, :]
            gelu = 0.5 * a * (1.0 + lax.erf(a * 0.7071067811865476))
            a_ref[lt, rows, :] = w * gelu.astype(BF16)

    for lt in range(n_lt):
        acc_ref[lt] += jnp.dot(vt_ref[...], a_ref[lt], preferred_element_type=F32)

    @pl.when(e == pl.num_programs(2) - 1)
    def _():
        m = mod_ref[0]
        for lt in range(n_lt):
            tok = slice(lt * PEER_LT, (lt + 1) * PEER_LT)
            y = acc_ref[lt].T
            o_ref[0, tok, :] = xmid_ref[0, tok, :] + m[5:6] * (_rms(y) * gpost_ref[...])


def _peer_call(hf, u_b, vt_b, s2t, e2t, cutt, e1t, x_mid, mod, g_post):
    b, s, _ = hf.shape
    tt = min(TT_PEER, s)
    nt = s // tt
    et = EI_PEER * N_KEYS
    ne = N_KEYS // EI_PEER
    tok3 = lambda i, j, e: (i, j, 0)
    lane_t = lambda i, j, e: (i * nt + j, 0, 0, 0)
    return pl.pallas_call(
        _peer_kernel,
        grid=(b, nt, ne),
        in_specs=[pl.BlockSpec((1, tt, D_MODEL), tok3),
                  pl.BlockSpec((et, D_MODEL), lambda i, j, e: (e, 0)),
                  pl.BlockSpec((D_MODEL, et), lambda i, j, e: (0, e)),
                  pl.BlockSpec((tt // PEER_LT, PEER_HEADS, N_KEYS, PEER_LT), lane_t),
                  pl.BlockSpec((tt // PEER_LT, PEER_HEADS, N_KEYS, PEER_LT), lane_t),
                  pl.BlockSpec((tt // PEER_LT, PEER_HEADS, EI_PEER, PEER_LT), lambda i, j, e: (i * nt + j, 0, e, 0)),
                  pl.BlockSpec((tt // PEER_LT, PEER_HEADS, EI_PEER, PEER_LT), lambda i, j, e: (i * nt + j, 0, e, 0)),
                  pl.BlockSpec((1, tt, D_MODEL), tok3),
                  pl.BlockSpec((1, 6, D_MODEL), lambda i, j, e: (i, 0, 0)),
                  pl.BlockSpec((1, D_MODEL), lambda i, j, e: (0, 0))],
        out_specs=pl.BlockSpec((1, tt, D_MODEL), tok3),
        out_shape=jax.ShapeDtypeStruct((b, s, D_MODEL), F32),
        scratch_shapes=[pltpu.VMEM((tt // PEER_LT, D_MODEL, PEER_LT), F32),
                        pltpu.VMEM((tt // PEER_LT, et, PEER_LT), F32),
                        pltpu.VMEM((tt // PEER_LT, et, PEER_LT), BF16)],
        compiler_params=pltpu.CompilerParams(
            vmem_limit_bytes=VMEM_LIMIT,
            dimension_semantics=("arbitrary", "arbitrary", "arbitrary")),
        name="peer",
    )(hf, u_b, vt_b, s2t, e2t, cutt, e1t, x_mid, mod, g_post)


def _rope_tables(seq):
    rows = seq // GRID_W
    row = jnp.broadcast_to(jnp.arange(rows, dtype=F32)[:, None], (rows, GRID_W)).reshape(-1)
    col = jnp.broadcast_to(jnp.arange(GRID_W, dtype=F32)[None, :], (rows, GRID_W)).reshape(-1)
    n_freq = ROPE // 4
    inv_freq = ROPE_BASE ** (-jnp.arange(n_freq, dtype=F32) / n_freq)
    ar = row[:, None] * inv_freq
    ac = col[:, None] * inv_freq
    cos = jnp.concatenate([jnp.cos(ar), jnp.cos(ar), jnp.cos(ac), jnp.cos(ac)], axis=1)
    sin = jnp.concatenate([-jnp.sin(ar), jnp.sin(ar), -jnp.sin(ac), jnp.sin(ac)], axis=1)
    return cos, sin


def _swap_halves(w):
    q = ROPE // 4
    return jnp.concatenate([w[..., q:2 * q], w[..., 0:q], w[..., 3 * q:4 * q], w[..., 2 * q:3 * q]], axis=-1)


def kernel(x, c, ctx, c_ctx, w_ada, b_ada, g_mix_pre, g_mix_post, g_ffn_pre, g_ffn_post, w_in, g_q_lora, w_uq,
           g_kv_lora, w_ukv, w_pool, pool_scale, w_out, peer_wq, peer_keys, peer_u, peer_v):
    assert w_ada.shape[0] == 1, "single-layer block"
    b, s, _ = x.shape
    layer = 0

    w_in_l = w_in[layer]
    k_rope_w = w_in_l[:, KV_LORA:KV_IN]
    win_r = jnp.concatenate([w_in_l[:, 0:KV_IN], _swap_halves(k_rope_w), w_in_l[:, KV_IN:]], axis=1).astype(BF16)
    wuq_h = w_uq[layer].reshape(Q_LORA, N_HEADS, QK_HEAD)
    wuq_r = jnp.concatenate([wuq_h, _swap_halves(wuq_h[..., NOPE:])], axis=-1).reshape(Q_LORA, N_HEADS * QH_W)
    wuq_r = wuq_r.astype(BF16)
    wukv_h = w_ukv[layer].reshape(KV_LORA, N_HEADS, NOPE + VHEAD)
    wk_abs = jnp.transpose(wukv_h[..., 0:NOPE], (1, 2, 0)).astype(BF16)
    wv_up = jnp.transpose(wukv_h[..., NOPE:], (1, 0, 2)).astype(BF16)
    cos_t, sin_t = _rope_tables(s)
    u_b = peer_u[layer].astype(BF16)
    vt_b = peer_v[layer].T.astype(BF16)
    row = lambda v: v[layer][None, :]

    n_rows = -(-(b + 1) // 8) * 8
    cc = jnp.concatenate([c, c_ctx[None, :], jnp.zeros((n_rows - b - 1, D_MODEL), F32)], axis=0)
    mod_all = _mod_call(cc, w_ada[layer], b_ada[layer][None, :])
    mod = mod_all[:b].reshape(b, 6, D_MODEL)
    mod_c = mod_all[b].reshape(6, D_MODEL)

    k_ctx = _ctx_call(ctx, mod_c, row(g_mix_pre), win_r[:, 0:256], row(g_kv_lora))
    k_lat, q, pool_in, gates = _proj_call(x, mod, row(g_mix_pre), win_r, row(g_q_lora), wuq_r, row(g_kv_lora),
                                          wk_abs, cos_t, sin_t)
    k_all = jnp.concatenate([k_ctx, k_lat], axis=1)
    attn = _attn_call(q, k_all, wv_up)
    ypool = _pool_call(pool_in, w_pool[layer].astype(BF16), row(pool_scale))
    x_mid, hf, s2t, e2t, cutt, e1t = _merge_call(
        x, attn, ypool, gates, mod, w_out[layer].astype(BF16), row(g_mix_post), row(g_ffn_pre),
        peer_wq[layer].astype(BF16), peer_keys[layer].astype(BF16))
    return _peer_call(hf, u_b, vt_b, s2t, e2t, cutt, e1t, x_mid, mod, row(g_ffn_post))
```

```python
import functools

import jax
import jax.numpy as jnp
from jax import lax
from jax.experimental import pallas as pl
from jax.experimental.pallas import tpu as pltpu

F32 = jnp.float32
BF16 = jnp.bfloat16

D_MODEL = 1024
GRID_W = 64
N_HEADS = 8
NOPE = 128
ROPE = 64
VHEAD = 128
QK_HEAD = NOPE + ROPE
Q_LORA = 256
KV_LORA = 128
ROPE_BASE = 10000.0
ATTN_SCALE = QK_HEAD ** -0.5
Q_SCALE = ATTN_SCALE * 1.4426950408889634
POOL_GROUPS = 4
POOL_GROUP_DIM = D_MODEL // POOL_GROUPS
POOL_MAX_HALF = 1 << (POOL_GROUPS - 1)
KV_IN = KV_LORA + ROPE
PEER_HEADS = 8
N_KEYS = 128
PEER_HALF = 128
PEER_TOPK = 16
EPS = 1e-6

WIN_CQ = KV_IN + ROPE
WIN_SMALL = WIN_CQ + Q_LORA
N_MOD = 6
WIN_POOL = WIN_SMALL
WIN_GATE = WIN_POOL + D_MODEL
WIN_TOTAL = WIN_GATE + 2 * D_MODEL
QH_W = 256

LANE = 128
PEER_LT = 256
VMEM_LIMIT = 56 * 1024 * 1024

TS_PROJ = 512
TQ_ATTN = 256
PV_ROWS = 128
TS_MERGE = 256
TT_PEER = 1024
EI_PEER = 8
PEER_CHUNK = 256

_NT = (((1,), (1,)), ((), ()))


def _rms(x):
    return x * lax.rsqrt(jnp.mean(x * x, axis=-1, keepdims=True) + EPS)


def _mod_kernel(cc_ref, w_ref, b_ref, o_ref):
    cc = cc_ref[...]
    a = cc * jax.nn.sigmoid(cc)
    o_ref[...] = jnp.dot(a, w_ref[...], preferred_element_type=F32) + b_ref[...]


def _mod_call(cc, w_ada, b_ada):
    rows = cc.shape[0]
    n = w_ada.shape[1]
    bn = 1024
    return pl.pallas_call(
        _mod_kernel,
        grid=(n // bn,),
        in_specs=[pl.BlockSpec((rows, D_MODEL), lambda j: (0, 0)),
                  pl.BlockSpec((D_MODEL, bn), lambda j: (0, j)),
                  pl.BlockSpec((1, bn), lambda j: (0, j))],
        out_specs=pl.BlockSpec((rows, bn), lambda j: (0, j)),
        out_shape=jax.ShapeDtypeStruct((rows, n), F32),
        compiler_params=pltpu.CompilerParams(vmem_limit_bytes=VMEM_LIMIT),
        name="mod",
    )(cc, w_ada, b_ada)


def _ctx_kernel(ctx_ref, mod_ref, gpre_ref, win_ref, gkv_ref, o_ref):
    x = ctx_ref[0]
    m = mod_ref[...]
    h = _rms(x) * gpre_ref[...]
    h = h * (1.0 + m[1:2]) + m[0:1]
    p = jnp.dot(h.astype(BF16), win_ref[...], preferred_element_type=F32)
    ckv_n = _rms(p[:, :KV_LORA]) * gkv_ref[...]
    o_ref[0, :, 0:KV_LORA] = ckv_n.astype(BF16)
    o_ref[0, :, KV_LORA:KV_IN] = p[:, KV_LORA:KV_IN].astype(BF16)


def _ctx_call(ctx, mod_c, g_pre, win_r, g_kv):
    b, lc, _ = ctx.shape
    return pl.pallas_call(
        _ctx_kernel,
        grid=(b,),
        in_specs=[pl.BlockSpec((1, lc, D_MODEL), lambda i: (i, 0, 0)),
                  pl.BlockSpec((N_MOD, D_MODEL), lambda i: (0, 0)),
                  pl.BlockSpec((1, D_MODEL), lambda i: (0, 0)),
                  pl.BlockSpec((D_MODEL, WIN_CQ), lambda i: (0, 0)),
                  pl.BlockSpec((1, KV_LORA), lambda i: (0, 0))],
        out_specs=pl.BlockSpec((1, lc, KV_IN), lambda i: (i, 0, 0)),
        out_shape=jax.ShapeDtypeStruct((b, lc, KV_IN), BF16),
        compiler_params=pltpu.CompilerParams(vmem_limit_bytes=VMEM_LIMIT),
        name="ctx_kv",
    )(ctx, mod_c, g_pre, win_r, g_kv)


def _proj_kernel(x_ref, mod_ref, gpre_ref, win_ref, gq_ref, wuq_ref, gkv_ref, wk_ref, cos_ref, sin_ref,
                 kcat_ref, q_ref, pool_ref, gates_ref):
    x = x_ref[0]
    m = mod_ref[0]
    h = _rms(x) * gpre_ref[...]
    hb = (h * (1.0 + m[1:2]) + m[0:1]).astype(BF16)
    cos = cos_ref[...]
    sin = sin_ref[...]

    p0 = jnp.dot(hb, win_ref[:, 0:WIN_SMALL], preferred_element_type=F32)
    ckv_n = _rms(p0[:, 0:KV_LORA]) * gkv_ref[...]
    k_rot = p0[:, KV_LORA:KV_IN] * cos + p0[:, KV_IN:WIN_CQ] * sin
    kcat_ref[0, :, 0:KV_LORA] = ckv_n.astype(BF16)
    kcat_ref[0, :, KV_LORA:KV_IN] = k_rot.astype(BF16)

    cqn = (_rms(p0[:, WIN_CQ:WIN_SMALL]) * gq_ref[...]).astype(BF16)

    pool_ref[0] = jnp.dot(hb, win_ref[:, WIN_POOL:WIN_GATE], preferred_element_type=F32).astype(BF16)
    qh = [jnp.dot(cqn, wuq_ref[:, hd * QH_W:(hd + 1) * QH_W], preferred_element_type=F32)
          for hd in range(N_HEADS)]
    for half in range(2):
        lo = WIN_GATE + half * D_MODEL
        g = jnp.dot(hb, win_ref[:, lo:lo + D_MODEL], preferred_element_type=F32)
        gates_ref[0, :, half * D_MODEL:(half + 1) * D_MODEL] = jax.nn.sigmoid(g).astype(BF16)
    for hd in range(N_HEADS):
        q_abs = jnp.dot(qh[hd][:, 0:NOPE].astype(BF16), wk_ref[hd], preferred_element_type=F32)
        q_rot = qh[hd][:, NOPE:QK_HEAD] * cos + qh[hd][:, QK_HEAD:QH_W] * sin
        q_ref[0, hd, :, 0:KV_LORA] = (q_abs * Q_SCALE).astype(BF16)
        q_ref[0, hd, :, KV_LORA:KV_IN] = (q_rot * Q_SCALE).astype(BF16)


def _proj_call(x, mod, g_pre, win_r, g_q, wuq_r, g_kv, wk_abs, cos_t, sin_t):
    b, s, _ = x.shape
    ts = min(TS_PROJ, s)
    const2 = lambda i, j: (0, 0)
    return pl.pallas_call(
        _proj_kernel,
        grid=(b, s // ts),
        in_specs=[pl.BlockSpec((1, ts, D_MODEL), lambda i, j: (i, j, 0)),
                  pl.BlockSpec((1, N_MOD, D_MODEL), lambda i, j: (i, 0, 0)),
                  pl.BlockSpec((1, D_MODEL), const2),
                  pl.BlockSpec((D_MODEL, WIN_TOTAL), const2),
                  pl.BlockSpec((1, Q_LORA), const2),
                  pl.BlockSpec((Q_LORA, N_HEADS * QH_W), const2),
                  pl.BlockSpec((1, KV_LORA), const2),
                  pl.BlockSpec((N_HEADS, NOPE, KV_LORA), lambda i, j: (0, 0, 0)),
                  pl.BlockSpec((ts, ROPE), lambda i, j: (j, 0)),
                  pl.BlockSpec((ts, ROPE), lambda i, j: (j, 0))],
        out_specs=[pl.BlockSpec((1, ts, KV_IN), lambda i, j: (i, j, 0)),
                   pl.BlockSpec((1, N_HEADS, ts, KV_IN), lambda i, j: (i, 0, j, 0)),
                   pl.BlockSpec((1, ts, D_MODEL), lambda i, j: (i, j, 0)),
                   pl.BlockSpec((1, ts, 2 * D_MODEL), lambda i, j: (i, j, 0))],
        out_shape=[jax.ShapeDtypeStruct((b, s, KV_IN), BF16),
                   jax.ShapeDtypeStruct((b, N_HEADS, s, KV_IN), BF16),
                   jax.ShapeDtypeStruct((b, s, D_MODEL), BF16),
                   jax.ShapeDtypeStruct((b, s, 2 * D_MODEL), BF16)],
        compiler_params=pltpu.CompilerParams(vmem_limit_bytes=VMEM_LIMIT),
        name="proj",
    )(x, mod, g_pre, win_r, g_q, wuq_r, g_kv, wk_abs, cos_t, sin_t)


def _attn_kernel(q_ref, k_ref, wv_ref, o_ref, s_ref, p_ref):
    k = k_ref[0]
    v = k[:, 0:KV_LORA]

    def scores(hd):
        s_ref[hd % 2] = lax.dot_general(q_ref[0, hd], k, _NT, preferred_element_type=F32)

    def project(hd, o):
        a = jnp.dot(o, wv_ref[hd], preferred_element_type=F32)
        o_ref[0, :, hd * VHEAD:(hd + 1) * VHEAD] = a.astype(BF16)

    scores(0)
    pending = None
    for hd in range(N_HEADS):
        par = hd % 2
        if hd + 1 < N_HEADS:
            scores(hd + 1)
        s = s_ref[par]
        p = jnp.exp2(s - jnp.max(s, axis=-1, keepdims=True))
        l = jnp.sum(p, axis=-1, keepdims=True)
        p_ref[par] = p.astype(BF16)
        o = jnp.concatenate([jnp.dot(p_ref[par, r0:r0 + PV_ROWS, :], v, preferred_element_type=F32)
                             for r0 in range(0, s.shape[0], PV_ROWS)], axis=0) / l
        if pending is not None:
            project(*pending)
        pending = (hd, o.astype(BF16))
    project(*pending)


def _attn_call(q, kall, wv):
    b, _, s, _ = q.shape
    lk = kall.shape[1]
    tq = min(TQ_ATTN, s)
    return pl.pallas_call(
        _attn_kernel,
        grid=(b, s // tq),
        in_specs=[pl.BlockSpec((1, N_HEADS, tq, KV_IN), lambda i, j: (i, 0, j, 0)),
                  pl.BlockSpec((1, lk, KV_IN), lambda i, j: (i, 0, 0)),
                  pl.BlockSpec((N_HEADS, KV_LORA, VHEAD), lambda i, j: (0, 0, 0))],
        out_specs=pl.BlockSpec((1, tq, D_MODEL), lambda i, j: (i, j, 0)),
        out_shape=jax.ShapeDtypeStruct((b, s, D_MODEL), BF16),
        scratch_shapes=[pltpu.VMEM((2, tq, lk), F32), pltpu.VMEM((2, tq, lk), BF16)],
        compiler_params=pltpu.CompilerParams(vmem_limit_bytes=VMEM_LIMIT),
        name="attn",
    )(q, kall, wv)


def _pool_kernel(xp_ref, w_ref, ps_ref, o_ref):
    g = pl.program_id(1)
    x = xp_ref[0].astype(F32)
    n = x.shape[0]
    t = lax.broadcasted_iota(jnp.int32, x.shape, 0)
    pad = POOL_MAX_HALF
    zeros = jnp.zeros((pad, x.shape[1]), F32)
    xe = jnp.concatenate([zeros, x, zeros], axis=0)
    ne = n + 2 * pad

    def down(a, k):
        return pltpu.roll(a, k, 0)

    def up(a, k):
        return pltpu.roll(a, ne - k, 0)

    w = down(xe, 1) + xe
    for lvl in range(1, POOL_GROUPS):
        k = 1 << (lvl - 1)
        w = jnp.where(g >= lvl, down(w, k) + up(w, k), w)
    w = w[pad:pad + n]
    half = lax.shift_left(jnp.int32(1), g)
    count = (jnp.minimum(t + half, n) - jnp.maximum(t - half, 0)).astype(F32)
    pooled = (w / count - x).astype(BF16)
    y = jnp.dot(pooled, w_ref[0], preferred_element_type=F32) * ps_ref[...]
    o_ref[0] = y.astype(BF16)


def _pool_call(pool_in, w_pool, pool_scale):
    b, s, _ = pool_in.shape
    gd = POOL_GROUP_DIM
    return pl.pallas_call(
        _pool_kernel,
        grid=(b, POOL_GROUPS),
        in_specs=[pl.BlockSpec((1, s, gd), lambda i, g: (i, 0, g)),
                  pl.BlockSpec((1, gd, gd), lambda i, g: (g, 0, 0)),
                  pl.BlockSpec((1, gd), lambda i, g: (0, g))],
        out_specs=pl.BlockSpec((1, s, gd), lambda i, g: (i, 0, g)),
        out_shape=jax.ShapeDtypeStruct((b, s, D_MODEL), BF16),
        compiler_params=pltpu.CompilerParams(vmem_limit_bytes=VMEM_LIMIT),
        name="pool",
    )(pool_in, w_pool, pool_scale)


SUB = 8


def _sort_network(n):
    pairs = []
    p = 1
    while p < PEER_TOPK:
        k = p
        while k >= 1:
            for j in range(k % p, PEER_TOPK - k, 2 * k):
                for i in range(min(k, PEER_TOPK - j - k)):
                    if (i + j) // (2 * p) == (i + j + k) // (2 * p):
                        pairs.append((i + j, i + j + k))
            k //= 2
        p *= 2
    return [(i, j) for i, j in pairs if j < n]


def _top16_sorted(vs, sort_last=True):
    vs = list(vs)
    for i, j in _sort_network(len(vs)):
        vs[i], vs[j] = jnp.maximum(vs[i], vs[j]), jnp.minimum(vs[i], vs[j])
    vs += [jnp.full(vs[0].shape, -jnp.inf, F32)] * (PEER_TOPK - len(vs))
    shift = SUB // 2
    while shift >= 1:
        other = [pltpu.roll(v, shift, 0) for v in vs]
        vs = [jnp.maximum(vs[k], other[PEER_TOPK - 1 - k]) for k in range(PEER_TOPK)]
        d = PEER_TOPK // 2 if (sort_last or shift > 1) else 0
        while d >= 1:
            for i in range(PEER_TOPK):
                if i & d == 0:
                    vs[i], vs[i + d] = jnp.maximum(vs[i], vs[i + d]), jnp.minimum(vs[i], vs[i + d])
            d //= 2
        shift //= 2
    return vs


def _peer_stats(s1, s2):
    lanes = s1.shape[1]
    n_v = N_KEYS // SUB
    rows = lax.broadcasted_iota(jnp.int32, (SUB, lanes), 0)
    v1 = [s1[SUB * k:SUB * (k + 1)] for k in range(n_v)]
    v2 = [s2[SUB * k:SUB * (k + 1)] for k in range(n_v)]
    a = _top16_sorted(v1)
    b = _top16_sorted(v2)

    def by_rank(vals):
        out = vals[0]
        for r in range(1, SUB):
            out = jnp.where(rows == r, vals[r], out)
        return out

    a_lo, a_hi, b_hi = by_rank(a[0:SUB]), by_rank(a[SUB:]), by_rank(b[SUB:])
    pieces = [a_lo + b[0], a_hi + b[0]] + [a_lo + b[c] for c in range(1, SUB)] + [b_hi + a[0]]
    thr = functools.reduce(jnp.minimum, _top16_sorted(pieces, sort_last=False))
    top = a[0] + b[0]
    sel = [p >= thr for p in pieces]
    z = sum(jnp.where(m, jnp.exp(p - top), 0.0) for m, p in zip(sel, pieces))
    inv_z = 1.0 / jnp.broadcast_to(jnp.sum(z, axis=0, keepdims=True), (SUB, lanes))
    one = [jnp.where(m, 1.0, 0.0) for m in sel]
    n_hi = one[1]
    n_lo = one[0] + sum(one[2:2 + SUB - 1])
    extra = jnp.broadcast_to(jnp.sum(one[-1], axis=0, keepdims=True), (SUB, lanes))
    n_lo = n_lo + jnp.where(rows == 0, extra, 0.0)
    n_of = [jnp.broadcast_to((n_lo if r < SUB else n_hi)[r % SUB:r % SUB + 1], (SUB, lanes))
            for r in range(PEER_TOPK)]
    cnt, e1, rank2, e2 = [], [], [], []
    for k in range(n_v):
        c = jnp.zeros((SUB, lanes), F32)
        rk = jnp.zeros((SUB, lanes), F32)
        for r in range(PEER_TOPK):
            c = jnp.where(v1[k] == a[r], n_of[r], c)
            rk = jnp.where(v2[k] < b[r], float(r + 1), rk)
        cnt.append(c)
        rank2.append(rk)
        e1.append(jnp.exp(v1[k] - a[0]) * inv_z)
        e2.append(jnp.exp(v2[k] - b[0]))
    cat = lambda vs: jnp.concatenate(vs, axis=0)
    return cat(cnt), cat(e1), cat(rank2), cat(e2)


def _merge_kernel(x_ref, attn_ref, yp_ref, gates_ref, mod_ref, wout_ref, gpost_ref, gffn_ref, wq_ref, keys_ref,
                  xmid_ref, hf_ref, rk_ref, e2_ref, cnt_ref, e1_ref):
    m = mod_ref[0]
    ga = gates_ref[0, :, 0:D_MODEL].astype(F32)
    gb = gates_ref[0, :, D_MODEL:2 * D_MODEL].astype(F32)
    merged = ga * attn_ref[0].astype(F32) + gb * yp_ref[0].astype(F32)
    y = jnp.dot(merged.astype(BF16), wout_ref[...], preferred_element_type=F32)
    x_mid = x_ref[0] + m[2:3] * (_rms(y) * gpost_ref[...])
    xmid_ref[0] = x_mid
    hf = (_rms(x_mid) * gffn_ref[...]) * (1.0 + m[4:5]) + m[3:4]
    hfb = hf.astype(BF16)
    hf_ref[0] = hfb
    q = jnp.dot(hfb, wq_ref[...], preferred_element_type=F32).astype(BF16)
    k1 = keys_ref[0]
    k2 = keys_ref[1]
    ts = q.shape[0]
    for hd in range(PEER_HEADS):
        base = hd * 2 * PEER_HALF
        s1 = lax.dot_general(k1, q[:, base:base + PEER_HALF], _NT, preferred_element_type=F32)
        s2 = lax.dot_general(k2, q[:, base + PEER_HALF:base + 2 * PEER_HALF], _NT, preferred_element_type=F32)
        for lt in range(ts // LANE):
            sl = slice(lt * LANE, (lt + 1) * LANE)
            blk, off = divmod(lt * LANE, PEER_LT)
            dst = slice(off, off + LANE)
            cnt, e1, rank2, e2 = _peer_stats(s1[:, sl], s2[:, sl])
            rk_ref[blk, hd, :, dst] = rank2.astype(BF16)
            e2_ref[blk, hd, :, dst] = e2.astype(BF16)
            cnt_ref[blk, hd, :, dst] = cnt
            e1_ref[blk, hd, :, dst] = e1


def _merge_call(x, attn, ypool, gates, mod, w_out, g_post, g_ffn, wq, keys):
    b, s, _ = x.shape
    ts = min(TS_MERGE, s)
    nt = s // ts
    n_lt = b * s // PEER_LT
    const2 = lambda i, j: (0, 0)
    tok3 = lambda i, j: (i, j, 0)
    lane_t = lambda i, j: (i * nt + j, 0, 0, 0)
    return pl.pallas_call(
        _merge_kernel,
        grid=(b, nt),
        in_specs=[pl.BlockSpec((1, ts, D_MODEL), tok3),
                  pl.BlockSpec((1, ts, D_MODEL), tok3),
                  pl.BlockSpec((1, ts, D_MODEL), tok3),
                  pl.BlockSpec((1, ts, 2 * D_MODEL), tok3),
                  pl.BlockSpec((1, N_MOD, D_MODEL), lambda i, j: (i, 0, 0)),
                  pl.BlockSpec((D_MODEL, D_MODEL), const2),
                  pl.BlockSpec((1, D_MODEL), const2),
                  pl.BlockSpec((1, D_MODEL), const2),
                  pl.BlockSpec((D_MODEL, PEER_HEADS * 2 * PEER_HALF), const2),
                  pl.BlockSpec((2, N_KEYS, PEER_HALF), lambda i, j: (0, 0, 0))],
        out_specs=[pl.BlockSpec((1, ts, D_MODEL), tok3),
                   pl.BlockSpec((1, ts, D_MODEL), tok3),
                   pl.BlockSpec((ts // PEER_LT, PEER_HEADS, N_KEYS, PEER_LT), lane_t),
                   pl.BlockSpec((ts // PEER_LT, PEER_HEADS, N_KEYS, PEER_LT), lane_t),
                   pl.BlockSpec((ts // PEER_LT, PEER_HEADS, N_KEYS, PEER_LT), lane_t),
                   pl.BlockSpec((ts // PEER_LT, PEER_HEADS, N_KEYS, PEER_LT), lane_t)],
        out_shape=[jax.ShapeDtypeStruct((b, s, D_MODEL), F32),
                   jax.ShapeDtypeStruct((b, s, D_MODEL), BF16),
                   jax.ShapeDtypeStruct((n_lt, PEER_HEADS, N_KEYS, PEER_LT), BF16),
                   jax.ShapeDtypeStruct((n_lt, PEER_HEADS, N_KEYS, PEER_LT), BF16),
                   jax.ShapeDtypeStruct((n_lt, PEER_HEADS, N_KEYS, PEER_LT), F32),
                   jax.ShapeDtypeStruct((n_lt, PEER_HEADS, N_KEYS, PEER_LT), F32)],
        compiler_params=pltpu.CompilerParams(vmem_limit_bytes=VMEM_LIMIT),
        name="merge",
    )(x, attn, ypool, gates, mod, w_out, g_post, g_ffn, wq, keys)


def _peer_kernel(hf_ref, u_ref, vt_ref, rk_ref, e2_ref, cnt_ref, e1_ref, xmid_ref, mod_ref, gpost_ref,
                 o_ref, acc_ref, act_ref, a_ref):
    e = pl.program_id(2)
    n_lt = hf_ref.shape[1] // PEER_LT

    @pl.when(e == 0)
    def _():
        acc_ref[...] = jnp.zeros_like(acc_ref)

    et = u_ref.shape[0]
    for lt in range(n_lt):
        hf_blk = hf_ref[0, lt * PEER_LT:(lt + 1) * PEER_LT, :]
        for r0 in range(0, et, PEER_CHUNK):
            act_ref[lt, r0:r0 + PEER_CHUNK, :] = lax.dot_general(
                u_ref[r0:r0 + PEER_CHUNK, :], hf_blk, _NT, preferred_element_type=F32)

    for lt in range(n_lt):
        cnt_hd = [cnt_ref[lt, hd].astype(BF16) for hd in range(PEER_HEADS)]
        e1_hd = [e1_ref[lt, hd].astype(BF16) for hd in range(PEER_HEADS)]
        for i in range(EI_PEER):
            w = jnp.zeros((N_KEYS, PEER_LT), BF16)
            for hd in range(PEER_HEADS):
                e2 = e2_ref[lt, hd]
                hit = rk_ref[lt, hd] < cnt_hd[hd][i:i + 1]
                w = w + jnp.where(hit, e2, jnp.zeros_like(e2)) * e1_hd[hd][i:i + 1]
            rows = slice(i * N_KEYS, (i + 1) * N_KEYS)
            a = act_ref[lt, rows, :].astype(BF16)
            gelu = (0.5 * a) * (1.0 + lax.erf(a * 0.7071067811865476))
            a_ref[lt, rows, :] = w * gelu

    for lt in range(n_lt):
        for r0 in range(0, et, PEER_CHUNK):
            acc_ref[lt] += jnp.dot(vt_ref[:, r0:r0 + PEER_CHUNK], a_ref[lt, r0:r0 + PEER_CHUNK, :],
                                   preferred_element_type=F32)

    @pl.when(e == pl.num_programs(2) - 1)
    def _():
        m = mod_ref[0]
        for lt in range(n_lt):
            tok = slice(lt * PEER_LT, (lt + 1) * PEER_LT)
            y = acc_ref[lt].T
            o_ref[0, tok, :] = xmid_ref[0, tok, :] + m[5:6] * (_rms(y) * gpost_ref[...])


def _peer_call(hf, u_b, vt_b, s2t, e2t, cutt, e1t, x_mid, mod, g_post):
    b, s, _ = hf.shape
    tt = min(TT_PEER, s)
    nt = s // tt
    et = EI_PEER * N_KEYS
    ne = N_KEYS // EI_PEER
    tok3 = lambda i, j, e: (i, j, 0)
    lane_t = lambda i, j, e: (i * nt + j, 0, 0, 0)
    return pl.pallas_call(
        _peer_kernel,
        grid=(b, nt, ne),
        in_specs=[pl.BlockSpec((1, tt, D_MODEL), tok3),
                  pl.BlockSpec((et, D_MODEL), lambda i, j, e: (e, 0)),
                  pl.BlockSpec((D_MODEL, et), lambda i, j, e: (0, e)),
                  pl.BlockSpec((tt // PEER_LT, PEER_HEADS, N_KEYS, PEER_LT), lane_t),
                  pl.BlockSpec((tt // PEER_LT, PEER_HEADS, N_KEYS, PEER_LT), lane_t),
                  pl.BlockSpec((tt // PEER_LT, PEER_HEADS, EI_PEER, PEER_LT), lambda i, j, e: (i * nt + j, 0, e, 0)),
                  pl.BlockSpec((tt // PEER_LT, PEER_HEADS, EI_PEER, PEER_LT), lambda i, j, e: (i * nt + j, 0, e, 0)),
                  pl.BlockSpec((1, tt, D_MODEL), tok3),
                  pl.BlockSpec((1, N_MOD, D_MODEL), lambda i, j, e: (i, 0, 0)),
                  pl.BlockSpec((1, D_MODEL), lambda i, j, e: (0, 0))],
        out_specs=pl.BlockSpec((1, tt, D_MODEL), tok3),
        out_shape=jax.ShapeDtypeStruct((b, s, D_MODEL), F32),
        scratch_shapes=[pltpu.VMEM((tt // PEER_LT, D_MODEL, PEER_LT), F32),
                        pltpu.VMEM((tt // PEER_LT, et, PEER_LT), F32),
                        pltpu.VMEM((tt // PEER_LT, et, PEER_LT), BF16)],
        compiler_params=pltpu.CompilerParams(
            vmem_limit_bytes=VMEM_LIMIT,
            dimension_semantics=("arbitrary", "arbitrary", "arbitrary")),
        name="peer",
    )(hf, u_b, vt_b, s2t, e2t, cutt, e1t, x_mid, mod, g_post)


def _rope_tables(seq):
    rows = seq // GRID_W
    row = jnp.broadcast_to(jnp.arange(rows, dtype=F32)[:, None], (rows, GRID_W)).reshape(-1)
    col = jnp.broadcast_to(jnp.arange(GRID_W, dtype=F32)[None, :], (rows, GRID_W)).reshape(-1)
    n_freq = ROPE // 4
    inv_freq = ROPE_BASE ** (-jnp.arange(n_freq, dtype=F32) / n_freq)
    ar = row[:, None] * inv_freq
    ac = col[:, None] * inv_freq
    cos = jnp.concatenate([jnp.cos(ar), jnp.cos(ar), jnp.cos(ac), jnp.cos(ac)], axis=1)
    sin = jnp.concatenate([-jnp.sin(ar), jnp.sin(ar), -jnp.sin(ac), jnp.sin(ac)], axis=1)
    return cos, sin


def _swap_halves(w):
    q = ROPE // 4
    return jnp.concatenate([w[..., q:2 * q], w[..., 0:q], w[..., 3 * q:4 * q], w[..., 2 * q:3 * q]], axis=-1)


def kernel(x, c, ctx, c_ctx, w_ada, b_ada, g_mix_pre, g_mix_post, g_ffn_pre, g_ffn_post, w_in, g_q_lora, w_uq,
           g_kv_lora, w_ukv, w_pool, pool_scale, w_out, peer_wq, peer_keys, peer_u, peer_v):
    assert w_ada.shape[0] == 1, "single-layer block"
    b, s, _ = x.shape
    layer = 0

    w_in_l = w_in[layer]
    k_rope_w = w_in_l[:, KV_LORA:KV_IN]
    win_r = jnp.concatenate([w_in_l[:, 0:KV_IN], _swap_halves(k_rope_w), w_in_l[:, KV_IN:]], axis=1).astype(BF16)
    wuq_h = w_uq[layer].reshape(Q_LORA, N_HEADS, QK_HEAD)
    wuq_r = jnp.concatenate([wuq_h, _swap_halves(wuq_h[..., NOPE:])], axis=-1).reshape(Q_LORA, N_HEADS * QH_W)
    wuq_r = wuq_r.astype(BF16)
    wukv_h = w_ukv[layer].reshape(KV_LORA, N_HEADS, NOPE + VHEAD)
    wk_abs = jnp.transpose(wukv_h[..., 0:NOPE], (1, 2, 0)).astype(BF16)
    wv_up = jnp.transpose(wukv_h[..., NOPE:], (1, 0, 2)).astype(BF16)
    cos_t, sin_t = _rope_tables(s)
    u_b = peer_u[layer].astype(BF16)
    vt_b = peer_v[layer].T.astype(BF16)
    row = lambda v: v[layer][None, :]

    n_rows = -(-(b + 1) // 8) * 8
    cc = jnp.concatenate([c, c_ctx[None, :], jnp.zeros((n_rows - b - 1, D_MODEL), F32)], axis=0)
    mod_all = _mod_call(cc, w_ada[layer], b_ada[layer][None, :])
    mod = mod_all[:b].reshape(b, N_MOD, D_MODEL)
    mod_c = mod_all[b].reshape(N_MOD, D_MODEL)

    k_ctx = _ctx_call(ctx, mod_c, row(g_mix_pre), win_r[:, 0:WIN_CQ], row(g_kv_lora))
    k_lat, q, pool_in, gates = _proj_call(x, mod, row(g_mix_pre), win_r, row(g_q_lora), wuq_r, row(g_kv_lora),
                                          wk_abs, cos_t, sin_t)
    k_all = jnp.concatenate([k_ctx, k_lat], axis=1)
    attn = _attn_call(q, k_all, wv_up)
    ypool = _pool_call(pool_in, w_pool[layer].astype(BF16), row(pool_scale))
    x_mid, hf, s2t, e2t, cutt, e1t = _merge_call(
        x, attn, ypool, gates, mod, w_out[layer].astype(BF16), row(g_mix_post), row(g_ffn_pre),
        peer_wq[layer].astype(BF16), peer_keys[layer].astype(BF16))
    return _peer_call(hf, u_b, vt_b, s2t, e2t, cutt, e1t, x_mid, mod, row(g_ffn_post))
```

```python
import functools

import jax
import jax.numpy as jnp
from jax import lax
from jax.experimental import pallas as pl
from jax.experimental.pallas import tpu as pltpu

F32 = jnp.float32
BF16 = jnp.bfloat16

D_MODEL = 1024
GRID_W = 64
N_HEADS = 8
NOPE = 128
ROPE = 64
VHEAD = 128
QK_HEAD = NOPE + ROPE
Q_LORA = 256
KV_LORA = 128
ROPE_BASE = 10000.0
ATTN_SCALE = QK_HEAD ** -0.5
Q_SCALE = ATTN_SCALE * 1.4426950408889634
POOL_GROUPS = 4
POOL_GROUP_DIM = D_MODEL // POOL_GROUPS
POOL_MAX_HALF = 1 << (POOL_GROUPS - 1)
KV_IN = KV_LORA + ROPE
PEER_HEADS = 8
N_KEYS = 128
PEER_HALF = 128
PEER_TOPK = 16
EPS = 1e-6

WIN_CQ = KV_IN + ROPE
WIN_SMALL = WIN_CQ + Q_LORA
N_MOD = 6
WIN_POOL = WIN_SMALL
WIN_GATE = WIN_POOL + D_MODEL
WIN_TOTAL = WIN_GATE + 2 * D_MODEL
QH_W = 256

LANE = 128
PEER_LT = 256
VMEM_LIMIT = 56 * 1024 * 1024

TS_PROJ = 512
TQ_ATTN = 256
PV_ROWS = 128
TS_MERGE = 256
TT_PEER = 1024
EI_PEER = 8
PEER_CHUNK = 256

_NT = (((1,), (1,)), ((), ()))


def _rms(x):
    return x * lax.rsqrt(jnp.mean(x * x, axis=-1, keepdims=True) + EPS)


def _mod_kernel(cc_ref, w_ref, b_ref, o_ref):
    cc = cc_ref[...]
    a = cc * jax.nn.sigmoid(cc)
    o_ref[...] = jnp.dot(a, w_ref[...], preferred_element_type=F32) + b_ref[...]


def _mod_call(cc, w_ada, b_ada):
    rows = cc.shape[0]
    n = w_ada.shape[1]
    bn = 1024
    return pl.pallas_call(
        _mod_kernel,
        grid=(n // bn,),
        in_specs=[pl.BlockSpec((rows, D_MODEL), lambda j: (0, 0)),
                  pl.BlockSpec((D_MODEL, bn), lambda j: (0, j)),
                  pl.BlockSpec((1, bn), lambda j: (0, j))],
        out_specs=pl.BlockSpec((rows, bn), lambda j: (0, j)),
        out_shape=jax.ShapeDtypeStruct((rows, n), F32),
        compiler_params=pltpu.CompilerParams(vmem_limit_bytes=VMEM_LIMIT),
        name="mod",
    )(cc, w_ada, b_ada)


def _ctx_kernel(ctx_ref, mod_ref, gpre_ref, win_ref, gkv_ref, o_ref):
    x = ctx_ref[0]
    m = mod_ref[...]
    h = _rms(x) * gpre_ref[...]
    h = h * (1.0 + m[1:2]) + m[0:1]
    p = jnp.dot(h.astype(BF16), win_ref[...], preferred_element_type=F32)
    ckv_n = _rms(p[:, :KV_LORA]) * gkv_ref[...]
    o_ref[0, :, 0:KV_LORA] = ckv_n.astype(BF16)
    o_ref[0, :, KV_LORA:KV_IN] = p[:, KV_LORA:KV_IN].astype(BF16)


def _ctx_call(ctx, mod_c, g_pre, win_r, g_kv):
    b, lc, _ = ctx.shape
    return pl.pallas_call(
        _ctx_kernel,
        grid=(b,),
        in_specs=[pl.BlockSpec((1, lc, D_MODEL), lambda i: (i, 0, 0)),
                  pl.BlockSpec((N_MOD, D_MODEL), lambda i: (0, 0)),
                  pl.BlockSpec((1, D_MODEL), lambda i: (0, 0)),
                  pl.BlockSpec((D_MODEL, WIN_CQ), lambda i: (0, 0)),
                  pl.BlockSpec((1, KV_LORA), lambda i: (0, 0))],
        out_specs=pl.BlockSpec((1, lc, KV_IN), lambda i: (i, 0, 0)),
        out_shape=jax.ShapeDtypeStruct((b, lc, KV_IN), BF16),
        compiler_params=pltpu.CompilerParams(vmem_limit_bytes=VMEM_LIMIT),
        name="ctx_kv",
    )(ctx, mod_c, g_pre, win_r, g_kv)


def _proj_kernel(x_ref, mod_ref, gpre_ref, win_ref, gq_ref, wuq_ref, gkv_ref, wk_ref, cos_ref, sin_ref,
                 kcat_ref, q_ref, pool_ref, gates_ref):
    x = x_ref[0]
    m = mod_ref[0]
    h = _rms(x) * gpre_ref[...]
    hb = (h * (1.0 + m[1:2]) + m[0:1]).astype(BF16)
    cos = cos_ref[...]
    sin = sin_ref[...]

    p0 = jnp.dot(hb, win_ref[:, 0:WIN_SMALL], preferred_element_type=F32)
    ckv_n = _rms(p0[:, 0:KV_LORA]) * gkv_ref[...]
    k_rot = p0[:, KV_LORA:KV_IN] * cos + p0[:, KV_IN:WIN_CQ] * sin
    kcat_ref[0, :, 0:KV_LORA] = ckv_n.astype(BF16)
    kcat_ref[0, :, KV_LORA:KV_IN] = k_rot.astype(BF16)

    cqn = (_rms(p0[:, WIN_CQ:WIN_SMALL]) * gq_ref[...]).astype(BF16)

    pool_ref[0] = jnp.dot(hb, win_ref[:, WIN_POOL:WIN_GATE], preferred_element_type=F32).astype(BF16)
    qh = [jnp.dot(cqn, wuq_ref[:, hd * QH_W:(hd + 1) * QH_W], preferred_element_type=F32)
          for hd in range(N_HEADS)]
    for half in range(2):
        lo = WIN_GATE + half * D_MODEL
        g = jnp.dot(hb, win_ref[:, lo:lo + D_MODEL], preferred_element_type=F32)
        gates_ref[0, :, half * D_MODEL:(half + 1) * D_MODEL] = jax.nn.sigmoid(g).astype(BF16)
    for hd in range(N_HEADS):
        q_abs = jnp.dot(qh[hd][:, 0:NOPE].astype(BF16), wk_ref[hd], preferred_element_type=F32)
        q_rot = qh[hd][:, NOPE:QK_HEAD] * cos + qh[hd][:, QK_HEAD:QH_W] * sin
        q_ref[0, hd, :, 0:KV_LORA] = (q_abs * Q_SCALE).astype(BF16)
        q_ref[0, hd, :, KV_LORA:KV_IN] = (q_rot * Q_SCALE).astype(BF16)


def _proj_call(x, mod, g_pre, win_r, g_q, wuq_r, g_kv, wk_abs, cos_t, sin_t):
    b, s, _ = x.shape
    ts = min(TS_PROJ, s)
    const2 = lambda i, j: (0, 0)
    return pl.pallas_call(
        _proj_kernel,
        grid=(b, s // ts),
        in_specs=[pl.BlockSpec((1, ts, D_MODEL), lambda i, j: (i, j, 0)),
                  pl.BlockSpec((1, N_MOD, D_MODEL), lambda i, j: (i, 0, 0)),
                  pl.BlockSpec((1, D_MODEL), const2),
                  pl.BlockSpec((D_MODEL, WIN_TOTAL), const2),
                  pl.BlockSpec((1, Q_LORA), const2),
                  pl.BlockSpec((Q_LORA, N_HEADS * QH_W), const2),
                  pl.BlockSpec((1, KV_LORA), const2),
                  pl.BlockSpec((N_HEADS, NOPE, KV_LORA), lambda i, j: (0, 0, 0)),
                  pl.BlockSpec((ts, ROPE), lambda i, j: (j, 0)),
                  pl.BlockSpec((ts, ROPE), lambda i, j: (j, 0))],
        out_specs=[pl.BlockSpec((1, ts, KV_IN), lambda i, j: (i, j, 0)),
                   pl.BlockSpec((1, N_HEADS, ts, KV_IN), lambda i, j: (i, 0, j, 0)),
                   pl.BlockSpec((1, ts, D_MODEL), lambda i, j: (i, j, 0)),
                   pl.BlockSpec((1, ts, 2 * D_MODEL), lambda i, j: (i, j, 0))],
        out_shape=[jax.ShapeDtypeStruct((b, s, KV_IN), BF16),
                   jax.ShapeDtypeStruct((b, N_HEADS, s, KV_IN), BF16),
                   jax.ShapeDtypeStruct((b, s, D_MODEL), BF16),
                   jax.ShapeDtypeStruct((b, s, 2 * D_MODEL), BF16)],
        compiler_params=pltpu.CompilerParams(vmem_limit_bytes=VMEM_LIMIT),
        name="proj",
    )(x, mod, g_pre, win_r, g_q, wuq_r, g_kv, wk_abs, cos_t, sin_t)


def _attn_kernel(q_ref, k_ref, wv_ref, o_ref, s_ref, p_ref):
    k = k_ref[0]
    v1 = jnp.concatenate([k[:, 0:KV_LORA], jnp.ones((k.shape[0], KV_LORA), BF16)], axis=1)

    def scores(hd):
        s_ref[hd % 2] = lax.dot_general(q_ref[0, hd], k, _NT, preferred_element_type=F32)

    def project(hd, o):
        a = jnp.dot(o, wv_ref[hd], preferred_element_type=F32)
        o_ref[0, :, hd * VHEAD:(hd + 1) * VHEAD] = a.astype(BF16)

    scores(0)
    pending = None
    for hd in range(N_HEADS):
        par = hd % 2
        if hd + 1 < N_HEADS:
            scores(hd + 1)
        s = s_ref[par]
        p_ref[par] = jnp.exp2(s - jnp.max(s, axis=-1, keepdims=True)).astype(BF16)
        ol = jnp.concatenate([jnp.dot(p_ref[par, r0:r0 + PV_ROWS, :], v1, preferred_element_type=F32)
                              for r0 in range(0, s.shape[0], PV_ROWS)], axis=0)
        o = ol[:, 0:KV_LORA] / ol[:, KV_LORA:KV_LORA + 1]
        if pending is not None:
            project(*pending)
        pending = (hd, o.astype(BF16))
    project(*pending)


def _attn_call(q, kall, wv):
    b, _, s, _ = q.shape
    lk = kall.shape[1]
    tq = min(TQ_ATTN, s)
    return pl.pallas_call(
        _attn_kernel,
        grid=(b, s // tq),
        in_specs=[pl.BlockSpec((1, N_HEADS, tq, KV_IN), lambda i, j: (i, 0, j, 0)),
                  pl.BlockSpec((1, lk, KV_IN), lambda i, j: (i, 0, 0)),
                  pl.BlockSpec((N_HEADS, KV_LORA, VHEAD), lambda i, j: (0, 0, 0))],
        out_specs=pl.BlockSpec((1, tq, D_MODEL), lambda i, j: (i, j, 0)),
        out_shape=jax.ShapeDtypeStruct((b, s, D_MODEL), BF16),
        scratch_shapes=[pltpu.VMEM((2, tq, lk), F32), pltpu.VMEM((2, tq, lk), BF16)],
        compiler_params=pltpu.CompilerParams(vmem_limit_bytes=VMEM_LIMIT),
        name="attn",
    )(q, kall, wv)


def _pool_kernel(xp_ref, w_ref, ps_ref, o_ref):
    g = pl.program_id(1)
    x = xp_ref[0].astype(F32)
    n = x.shape[0]
    t = lax.broadcasted_iota(jnp.int32, x.shape, 0)
    pad = POOL_MAX_HALF
    zeros = jnp.zeros((pad, x.shape[1]), F32)
    xe = jnp.concatenate([zeros, x, zeros], axis=0)
    ne = n + 2 * pad

    def down(a, k):
        return pltpu.roll(a, k, 0)

    def up(a, k):
        return pltpu.roll(a, ne - k, 0)

    w = down(xe, 1) + xe
    for lvl in range(1, POOL_GROUPS):
        k = 1 << (lvl - 1)
        w = jnp.where(g >= lvl, down(w, k) + up(w, k), w)
    w = w[pad:pad + n]
    half = lax.shift_left(jnp.int32(1), g)
    count = (jnp.minimum(t + half, n) - jnp.maximum(t - half, 0)).astype(F32)
    pooled = (w / count - x).astype(BF16)
    y = jnp.dot(pooled, w_ref[0], preferred_element_type=F32) * ps_ref[...]
    o_ref[0] = y.astype(BF16)


def _pool_call(pool_in, w_pool, pool_scale):
    b, s, _ = pool_in.shape
    gd = POOL_GROUP_DIM
    return pl.pallas_call(
        _pool_kernel,
        grid=(b, POOL_GROUPS),
        in_specs=[pl.BlockSpec((1, s, gd), lambda i, g: (i, 0, g)),
                  pl.BlockSpec((1, gd, gd), lambda i, g: (g, 0, 0)),
                  pl.BlockSpec((1, gd), lambda i, g: (0, g))],
        out_specs=pl.BlockSpec((1, s, gd), lambda i, g: (i, 0, g)),
        out_shape=jax.ShapeDtypeStruct((b, s, D_MODEL), BF16),
        compiler_params=pltpu.CompilerParams(vmem_limit_bytes=VMEM_LIMIT),
        name="pool",
    )(pool_in, w_pool, pool_scale)


SUB = 8


def _sort_network(n):
    pairs = []
    p = 1
    while p < PEER_TOPK:
        k = p
        while k >= 1:
            for j in range(k % p, PEER_TOPK - k, 2 * k):
                for i in range(min(k, PEER_TOPK - j - k)):
                    if (i + j) // (2 * p) == (i + j + k) // (2 * p):
                        pairs.append((i + j, i + j + k))
            k //= 2
        p *= 2
    return [(i, j) for i, j in pairs if j < n]


def _top16_sorted(vs, sort_last=True):
    vs = list(vs)
    for i, j in _sort_network(len(vs)):
        vs[i], vs[j] = jnp.maximum(vs[i], vs[j]), jnp.minimum(vs[i], vs[j])
    vs += [jnp.full(vs[0].shape, -jnp.inf, F32)] * (PEER_TOPK - len(vs))
    shift = SUB // 2
    while shift >= 1:
        other = [pltpu.roll(v, shift, 0) for v in vs]
        vs = [jnp.maximum(vs[k], other[PEER_TOPK - 1 - k]) for k in range(PEER_TOPK)]
        d = PEER_TOPK // 2 if (sort_last or shift > 1) else 0
        while d >= 1:
            for i in range(PEER_TOPK):
                if i & d == 0:
                    vs[i], vs[i + d] = jnp.maximum(vs[i], vs[i + d]), jnp.minimum(vs[i], vs[i + d])
            d //= 2
        shift //= 2
    return vs


def _peer_stats(s1, s2):
    lanes = s1.shape[1]
    n_v = N_KEYS // SUB
    rows = lax.broadcasted_iota(jnp.int32, (SUB, lanes), 0)
    v1 = [s1[SUB * k:SUB * (k + 1)] for k in range(n_v)]
    v2 = [s2[SUB * k:SUB * (k + 1)] for k in range(n_v)]
    a = _top16_sorted(v1)
    b = _top16_sorted(v2)

    def by_rank(vals):
        out = vals[0]
        for r in range(1, SUB):
            out = jnp.where(rows == r, vals[r], out)
        return out

    a_lo, a_hi, b_hi = by_rank(a[0:SUB]), by_rank(a[SUB:]), by_rank(b[SUB:])
    pieces = [a_lo + b[0], a_hi + b[0]] + [a_lo + b[c] for c in range(1, SUB)] + [b_hi + a[0]]
    thr = functools.reduce(jnp.minimum, _top16_sorted(pieces, sort_last=False))
    top = a[0] + b[0]
    sel = [p >= thr for p in pieces]
    z = sum(jnp.where(m, jnp.exp(p - top), 0.0) for m, p in zip(sel, pieces))
    inv_z = 1.0 / jnp.broadcast_to(jnp.sum(z, axis=0, keepdims=True), (SUB, lanes))
    one = [jnp.where(m, 1.0, 0.0) for m in sel]
    n_hi = one[1]
    n_lo = one[0] + sum(one[2:2 + SUB - 1])
    extra = jnp.broadcast_to(jnp.sum(one[-1], axis=0, keepdims=True), (SUB, lanes))
    n_lo = n_lo + jnp.where(rows == 0, extra, 0.0)
    n_of = [jnp.broadcast_to((n_lo if r < SUB else n_hi)[r % SUB:r % SUB + 1], (SUB, lanes))
            for r in range(PEER_TOPK)]
    cnt, e1, rank2, e2 = [], [], [], []
    for k in range(n_v):
        c = jnp.zeros((SUB, lanes), F32)
        rk = jnp.zeros((SUB, lanes), F32)
        for r in range(PEER_TOPK):
            c = jnp.where(v1[k] == a[r], n_of[r], c)
            rk = jnp.where(v2[k] < b[r], float(r + 1), rk)
        cnt.append(c)
        rank2.append(rk)
        e1.append(jnp.exp(v1[k] - a[0]) * inv_z)
        e2.append(jnp.exp(v2[k] - b[0]))
    cat = lambda vs: jnp.concatenate(vs, axis=0)
    return cat(cnt), cat(e1), cat(rank2), cat(e2)


def _merge_kernel(x_ref, attn_ref, yp_ref, gates_ref, mod_ref, wout_ref, gpost_ref, gffn_ref, wq_ref, keys_ref,
                  xmid_ref, hf_ref, rk_ref, e2_ref, cnt_ref, e1_ref):
    m = mod_ref[0]
    ga = gates_ref[0, :, 0:D_MODEL].astype(F32)
    gb = gates_ref[0, :, D_MODEL:2 * D_MODEL].astype(F32)
    merged = ga * attn_ref[0].astype(F32) + gb * yp_ref[0].astype(F32)
    y = jnp.dot(merged.astype(BF16), wout_ref[...], preferred_element_type=F32)
    x_mid = x_ref[0] + m[2:3] * (_rms(y) * gpost_ref[...])
    xmid_ref[0] = x_mid
    hf = (_rms(x_mid) * gffn_ref[...]) * (1.0 + m[4:5]) + m[3:4]
    hfb = hf.astype(BF16)
    hf_ref[0] = hfb
    q = jnp.dot(hfb, wq_ref[...], preferred_element_type=F32).astype(BF16)
    k1 = keys_ref[0]
    k2 = keys_ref[1]
    ts = q.shape[0]
    for hd in range(PEER_HEADS):
        base = hd * 2 * PEER_HALF
        s1 = lax.dot_general(k1, q[:, base:base + PEER_HALF], _NT, preferred_element_type=F32)
        s2 = lax.dot_general(k2, q[:, base + PEER_HALF:base + 2 * PEER_HALF], _NT, preferred_element_type=F32)
        for lt in range(ts // LANE):
            sl = slice(lt * LANE, (lt + 1) * LANE)
            blk, off = divmod(lt * LANE, PEER_LT)
            dst = slice(off, off + LANE)
            cnt, e1, rank2, e2 = _peer_stats(s1[:, sl], s2[:, sl])
            rk_ref[blk, hd, :, dst] = rank2.astype(BF16)
            e2_ref[blk, hd, :, dst] = e2.astype(BF16)
            cnt_ref[blk, hd, :, dst] = cnt
            e1_ref[blk, hd, :, dst] = e1


def _merge_call(x, attn, ypool, gates, mod, w_out, g_post, g_ffn, wq, keys):
    b, s, _ = x.shape
    ts = min(TS_MERGE, s)
    nt = s // ts
    n_lt = b * s // PEER_LT
    const2 = lambda i, j: (0, 0)
    tok3 = lambda i, j: (i, j, 0)
    lane_t = lambda i, j: (i * nt + j, 0, 0, 0)
    return pl.pallas_call(
        _merge_kernel,
        grid=(b, nt),
        in_specs=[pl.BlockSpec((1, ts, D_MODEL), tok3),
                  pl.BlockSpec((1, ts, D_MODEL), tok3),
                  pl.BlockSpec((1, ts, D_MODEL), tok3),
                  pl.BlockSpec((1, ts, 2 * D_MODEL), tok3),
                  pl.BlockSpec((1, N_MOD, D_MODEL), lambda i, j: (i, 0, 0)),
                  pl.BlockSpec((D_MODEL, D_MODEL), const2),
                  pl.BlockSpec((1, D_MODEL), const2),
                  pl.BlockSpec((1, D_MODEL), const2),
                  pl.BlockSpec((D_MODEL, PEER_HEADS * 2 * PEER_HALF), const2),
                  pl.BlockSpec((2, N_KEYS, PEER_HALF), lambda i, j: (0, 0, 0))],
        out_specs=[pl.BlockSpec((1, ts, D_MODEL), tok3),
                   pl.BlockSpec((1, ts, D_MODEL), tok3),
                   pl.BlockSpec((ts // PEER_LT, PEER_HEADS, N_KEYS, PEER_LT), lane_t),
                   pl.BlockSpec((ts // PEER_LT, PEER_HEADS, N_KEYS, PEER_LT), lane_t),
                   pl.BlockSpec((ts // PEER_LT, PEER_HEADS, N_KEYS, PEER_LT), lane_t),
                   pl.BlockSpec((ts // PEER_LT, PEER_HEADS, N_KEYS, PEER_LT), lane_t)],
        out_shape=[jax.ShapeDtypeStruct((b, s, D_MODEL), F32),
                   jax.ShapeDtypeStruct((b, s, D_MODEL), BF16),
                   jax.ShapeDtypeStruct((n_lt, PEER_HEADS, N_KEYS, PEER_LT), BF16),
                   jax.ShapeDtypeStruct((n_lt, PEER_HEADS, N_KEYS, PEER_LT), BF16),
                   jax.ShapeDtypeStruct((n_lt, PEER_HEADS, N_KEYS, PEER_LT), F32),
                   jax.ShapeDtypeStruct((n_lt, PEER_HEADS, N_KEYS, PEER_LT), F32)],
        compiler_params=pltpu.CompilerParams(vmem_limit_bytes=VMEM_LIMIT),
        name="merge",
    )(x, attn, ypool, gates, mod, w_out, g_post, g_ffn, wq, keys)


def _peer_kernel(hf_ref, u_ref, vt_ref, rk_ref, e2_ref, cnt_ref, e1_ref, xmid_ref, mod_ref, gpost_ref,
                 o_ref, acc_ref, act_ref, a_ref):
    e = pl.program_id(2)
    n_lt = hf_ref.shape[1] // PEER_LT

    @pl.when(e == 0)
    def _():
        acc_ref[...] = jnp.zeros_like(acc_ref)

    et = u_ref.shape[0]
    for lt in range(n_lt):
        hf_blk = hf_ref[0, lt * PEER_LT:(lt + 1) * PEER_LT, :]
        for r0 in range(0, et, PEER_CHUNK):
            act_ref[lt, r0:r0 + PEER_CHUNK, :] = lax.dot_general(
                u_ref[r0:r0 + PEER_CHUNK, :], hf_blk, _NT, preferred_element_type=F32)

    for lt in range(n_lt):
        cnt_hd = [cnt_ref[lt, hd].astype(BF16) for hd in range(PEER_HEADS)]
        e1_hd = [e1_ref[lt, hd].astype(BF16) for hd in range(PEER_HEADS)]
        for i in range(EI_PEER):
            w = jnp.zeros((N_KEYS, PEER_LT), BF16)
            for hd in range(PEER_HEADS):
                e2 = e2_ref[lt, hd]
                hit = rk_ref[lt, hd] < cnt_hd[hd][i:i + 1]
                w = w + jnp.where(hit, e2, jnp.zeros_like(e2)) * e1_hd[hd][i:i + 1]
            rows = slice(i * N_KEYS, (i + 1) * N_KEYS)
            a = act_ref[lt, rows, :].astype(BF16)
            gelu = (0.5 * a) * (1.0 + lax.erf(a * 0.7071067811865476))
            a_ref[lt, rows, :] = w * gelu

    for lt in range(n_lt):
        for r0 in range(0, et, PEER_CHUNK):
            acc_ref[lt] += jnp.dot(vt_ref[:, r0:r0 + PEER_CHUNK], a_ref[lt, r0:r0 + PEER_CHUNK, :],
                                   preferred_element_type=F32)

    @pl.when(e == pl.num_programs(2) - 1)
    def _():
        m = mod_ref[0]
        for lt in range(n_lt):
            tok = slice(lt * PEER_LT, (lt + 1) * PEER_LT)
            y = acc_ref[lt].T
            o_ref[0, tok, :] = xmid_ref[0, tok, :] + m[5:6] * (_rms(y) * gpost_ref[...])


def _peer_call(hf, u_b, vt_b, s2t, e2t, cutt, e1t, x_mid, mod, g_post):
    b, s, _ = hf.shape
    tt = min(TT_PEER, s)
    nt = s // tt
    et = EI_PEER * N_KEYS
    ne = N_KEYS // EI_PEER
    tok3 = lambda i, j, e: (i, j, 0)
    lane_t = lambda i, j, e: (i * nt + j, 0, 0, 0)
    return pl.pallas_call(
        _peer_kernel,
        grid=(b, nt, ne),
        in_specs=[pl.BlockSpec((1, tt, D_MODEL), tok3),
                  pl.BlockSpec((et, D_MODEL), lambda i, j, e: (e, 0)),
                  pl.BlockSpec((D_MODEL, et), lambda i, j, e: (0, e)),
                  pl.BlockSpec((tt // PEER_LT, PEER_HEADS, N_KEYS, PEER_LT), lane_t),
                  pl.BlockSpec((tt // PEER_LT, PEER_HEADS, N_KEYS, PEER_LT), lane_t),
                  pl.BlockSpec((tt // PEER_LT, PEER_HEADS, EI_PEER, PEER_LT), lambda i, j, e: (i * nt + j, 0, e, 0)),
                  pl.BlockSpec((tt // PEER_LT, PEER_HEADS, EI_PEER, PEER_LT), lambda i, j, e: (i * nt + j, 0, e, 0)),
                  pl.BlockSpec((1, tt, D_MODEL), tok3),
                  pl.BlockSpec((1, N_MOD, D_MODEL), lambda i, j, e: (i, 0, 0)),
                  pl.BlockSpec((1, D_MODEL), lambda i, j, e: (0, 0))],
        out_specs=pl.BlockSpec((1, tt, D_MODEL), tok3),
        out_shape=jax.ShapeDtypeStruct((b, s, D_MODEL), F32),
        scratch_shapes=[pltpu.VMEM((tt // PEER_LT, D_MODEL, PEER_LT), F32),
                        pltpu.VMEM((tt // PEER_LT, et, PEER_LT), F32),
                        pltpu.VMEM((tt // PEER_LT, et, PEER_LT), BF16)],
        compiler_params=pltpu.CompilerParams(
            vmem_limit_bytes=VMEM_LIMIT,
            dimension_semantics=("arbitrary", "arbitrary", "arbitrary")),
        name="peer",
    )(hf, u_b, vt_b, s2t, e2t, cutt, e1t, x_mid, mod, g_post)


def _rope_tables(seq):
    rows = seq // GRID_W
    row = jnp.broadcast_to(jnp.arange(rows, dtype=F32)[:, None], (rows, GRID_W)).reshape(-1)
    col = jnp.broadcast_to(jnp.arange(GRID_W, dtype=F32)[None, :], (rows, GRID_W)).reshape(-1)
    n_freq = ROPE // 4
    inv_freq = ROPE_BASE ** (-jnp.arange(n_freq, dtype=F32) / n_freq)
    ar = row[:, None] * inv_freq
    ac = col[:, None] * inv_freq
    cos = jnp.concatenate([jnp.cos(ar), jnp.cos(ar), jnp.cos(ac), jnp.cos(ac)], axis=1)
    sin = jnp.concatenate([-jnp.sin(ar), jnp.sin(ar), -jnp.sin(ac), jnp.sin(ac)], axis=1)
    return cos, sin


def _swap_halves(w):
    q = ROPE // 4
    return jnp.concatenate([w[..., q:2 * q], w[..., 0:q], w[..., 3 * q:4 * q], w[..., 2 * q:3 * q]], axis=-1)


def kernel(x, c, ctx, c_ctx, w_ada, b_ada, g_mix_pre, g_mix_post, g_ffn_pre, g_ffn_post, w_in, g_q_lora, w_uq,
           g_kv_lora, w_ukv, w_pool, pool_scale, w_out, peer_wq, peer_keys, peer_u, peer_v):
    assert w_ada.shape[0] == 1, "single-layer block"
    b, s, _ = x.shape
    layer = 0

    w_in_l = w_in[layer]
    k_rope_w = w_in_l[:, KV_LORA:KV_IN]
    win_r = jnp.concatenate([w_in_l[:, 0:KV_IN], _swap_halves(k_rope_w), w_in_l[:, KV_IN:]], axis=1).astype(BF16)
    wuq_h = w_uq[layer].reshape(Q_LORA, N_HEADS, QK_HEAD)
    wuq_r = jnp.concatenate([wuq_h, _swap_halves(wuq_h[..., NOPE:])], axis=-1).reshape(Q_LORA, N_HEADS * QH_W)
    wuq_r = wuq_r.astype(BF16)
    wukv_h = w_ukv[layer].reshape(KV_LORA, N_HEADS, NOPE + VHEAD)
    wk_abs = jnp.transpose(wukv_h[..., 0:NOPE], (1, 2, 0)).astype(BF16)
    wv_up = jnp.transpose(wukv_h[..., NOPE:], (1, 0, 2)).astype(BF16)
    cos_t, sin_t = _rope_tables(s)
    u_b = peer_u[layer].astype(BF16)
    vt_b = peer_v[layer].T.astype(BF16)
    row = lambda v: v[layer][None, :]

    n_rows = -(-(b + 1) // 8) * 8
    cc = jnp.concatenate([c, c_ctx[None, :], jnp.zeros((n_rows - b - 1, D_MODEL), F32)], axis=0)
    mod_all = _mod_call(cc, w_ada[layer], b_ada[layer][None, :])
    mod = mod_all[:b].reshape(b, N_MOD, D_MODEL)
    mod_c = mod_all[b].reshape(N_MOD, D_MODEL)

    k_ctx = _ctx_call(ctx, mod_c, row(g_mix_pre), win_r[:, 0:WIN_CQ], row(g_kv_lora))
    k_lat, q, pool_in, gates = _proj_call(x, mod, row(g_mix_pre), win_r, row(g_q_lora), wuq_r, row(g_kv_lora),
                                          wk_abs, cos_t, sin_t)
    k_all = jnp.concatenate([k_ctx, k_lat], axis=1)
    attn = _attn_call(q, k_all, wv_up)
    ypool = _pool_call(pool_in, w_pool[layer].astype(BF16), row(pool_scale))
    x_mid, hf, s2t, e2t, cutt, e1t = _merge_call(
        x, attn, ypool, gates, mod, w_out[layer].astype(BF16), row(g_mix_post), row(g_ffn_pre),
        peer_wq[layer].astype(BF16), peer_keys[layer].astype(BF16))
    return _peer_call(hf, u_b, vt_b, s2t, e2t, cutt, e1t, x_mid, mod, row(g_ffn_post))
```

```python
import functools

import jax
import jax.numpy as jnp
from jax import lax
from jax.experimental import pallas as pl
from jax.experimental.pallas import tpu as pltpu

F32 = jnp.float32
BF16 = jnp.bfloat16

D_MODEL = 1024
GRID_W = 64
N_HEADS = 8
NOPE = 128
ROPE = 64
VHEAD = 128
QK_HEAD = NOPE + ROPE
Q_LORA = 256
KV_LORA = 128
ROPE_BASE = 10000.0
ATTN_SCALE = QK_HEAD ** -0.5
Q_SCALE = ATTN_SCALE * 1.4426950408889634
POOL_GROUPS = 4
POOL_GROUP_DIM = D_MODEL // POOL_GROUPS
POOL_MAX_HALF = 1 << (POOL_GROUPS - 1)
KV_IN = KV_LORA + ROPE
PEER_HEADS = 8
N_KEYS = 128
PEER_HALF = 128
PEER_TOPK = 16
EPS = 1e-6

WIN_CQ = KV_IN + ROPE
WIN_SMALL = WIN_CQ + Q_LORA
N_MOD = 6
WIN_POOL = WIN_SMALL
WIN_GATE = WIN_POOL + D_MODEL
WIN_TOTAL = WIN_GATE + 2 * D_MODEL
QH_W = 256

LANE = 128
PEER_LT = 256
VMEM_LIMIT = 56 * 1024 * 1024

TS_PROJ = 512
TQ_ATTN = 512
PV_ROWS = 128
TS_MERGE = 256
TT_PEER = 1024
EI_PEER = 8
PEER_CHUNK = 256

_NT = (((1,), (1,)), ((), ()))


def _rms(x):
    return x * lax.rsqrt(jnp.mean(x * x, axis=-1, keepdims=True) + EPS)


def _mod_kernel(cc_ref, w_ref, b_ref, o_ref):
    cc = cc_ref[...]
    a = cc * jax.nn.sigmoid(cc)
    o_ref[...] = jnp.dot(a, w_ref[...], preferred_element_type=F32) + b_ref[...]


def _mod_call(cc, w_ada, b_ada):
    rows = cc.shape[0]
    n = w_ada.shape[1]
    bn = 1024
    return pl.pallas_call(
        _mod_kernel,
        grid=(n // bn,),
        in_specs=[pl.BlockSpec((rows, D_MODEL), lambda j: (0, 0)),
                  pl.BlockSpec((D_MODEL, bn), lambda j: (0, j)),
                  pl.BlockSpec((1, bn), lambda j: (0, j))],
        out_specs=pl.BlockSpec((rows, bn), lambda j: (0, j)),
        out_shape=jax.ShapeDtypeStruct((rows, n), F32),
        compiler_params=pltpu.CompilerParams(vmem_limit_bytes=VMEM_LIMIT),
        name="mod",
    )(cc, w_ada, b_ada)


def _ctx_kernel(ctx_ref, mod_ref, gpre_ref, win_ref, gkv_ref, o_ref):
    x = ctx_ref[0]
    m = mod_ref[...]
    h = _rms(x) * gpre_ref[...]
    h = h * (1.0 + m[1:2]) + m[0:1]
    p = jnp.dot(h.astype(BF16), win_ref[...], preferred_element_type=F32)
    ckv_n = _rms(p[:, :KV_LORA]) * gkv_ref[...]
    o_ref[0, :, 0:KV_LORA] = ckv_n.astype(BF16)
    o_ref[0, :, KV_LORA:KV_IN] = p[:, KV_LORA:KV_IN].astype(BF16)


def _ctx_call(ctx, mod_c, g_pre, win_r, g_kv):
    b, lc, _ = ctx.shape
    return pl.pallas_call(
        _ctx_kernel,
        grid=(b,),
        in_specs=[pl.BlockSpec((1, lc, D_MODEL), lambda i: (i, 0, 0)),
                  pl.BlockSpec((N_MOD, D_MODEL), lambda i: (0, 0)),
                  pl.BlockSpec((1, D_MODEL), lambda i: (0, 0)),
                  pl.BlockSpec((D_MODEL, WIN_CQ), lambda i: (0, 0)),
                  pl.BlockSpec((1, KV_LORA), lambda i: (0, 0))],
        out_specs=pl.BlockSpec((1, lc, KV_IN), lambda i: (i, 0, 0)),
        out_shape=jax.ShapeDtypeStruct((b, lc, KV_IN), BF16),
        compiler_params=pltpu.CompilerParams(vmem_limit_bytes=VMEM_LIMIT),
        name="ctx_kv",
    )(ctx, mod_c, g_pre, win_r, g_kv)


def _proj_kernel(x_ref, mod_ref, gpre_ref, win_ref, gq_ref, wuq_ref, gkv_ref, wk_ref, cos_ref, sin_ref,
                 kcat_ref, q_ref, pool_ref, gates_ref):
    x = x_ref[0]
    m = mod_ref[0]
    h = _rms(x) * gpre_ref[...]
    hb = (h * (1.0 + m[1:2]) + m[0:1]).astype(BF16)
    cos = cos_ref[...]
    sin = sin_ref[...]

    p0 = jnp.dot(hb, win_ref[:, 0:WIN_SMALL], preferred_element_type=F32)
    ckv_n = _rms(p0[:, 0:KV_LORA]) * gkv_ref[...]
    k_rot = p0[:, KV_LORA:KV_IN] * cos + p0[:, KV_IN:WIN_CQ] * sin
    kcat_ref[0, :, 0:KV_LORA] = ckv_n.astype(BF16)
    kcat_ref[0, :, KV_LORA:KV_IN] = k_rot.astype(BF16)

    cqn = (_rms(p0[:, WIN_CQ:WIN_SMALL]) * gq_ref[...]).astype(BF16)

    pool_ref[0] = jnp.dot(hb, win_ref[:, WIN_POOL:WIN_GATE], preferred_element_type=F32).astype(BF16)
    qh = [jnp.dot(cqn, wuq_ref[:, hd * QH_W:(hd + 1) * QH_W], preferred_element_type=F32)
          for hd in range(N_HEADS)]
    for half in range(2):
        lo = WIN_GATE + half * D_MODEL
        g = jnp.dot(hb, win_ref[:, lo:lo + D_MODEL], preferred_element_type=F32)
        gates_ref[0, :, half * D_MODEL:(half + 1) * D_MODEL] = jax.nn.sigmoid(g).astype(BF16)
    for hd in range(N_HEADS):
        q_abs = jnp.dot(qh[hd][:, 0:NOPE].astype(BF16), wk_ref[hd], preferred_element_type=F32)
        q_rot = qh[hd][:, NOPE:QK_HEAD] * cos + qh[hd][:, QK_HEAD:QH_W] * sin
        q_ref[0, hd, :, 0:KV_LORA] = (q_abs * Q_SCALE).astype(BF16)
        q_ref[0, hd, :, KV_LORA:KV_IN] = (q_rot * Q_SCALE).astype(BF16)


def _proj_call(x, mod, g_pre, win_r, g_q, wuq_r, g_kv, wk_abs, cos_t, sin_t):
    b, s, _ = x.shape
    ts = min(TS_PROJ, s)
    const2 = lambda i, j: (0, 0)
    return pl.pallas_call(
        _proj_kernel,
        grid=(b, s // ts),
        in_specs=[pl.BlockSpec((1, ts, D_MODEL), lambda i, j: (i, j, 0)),
                  pl.BlockSpec((1, N_MOD, D_MODEL), lambda i, j: (i, 0, 0)),
                  pl.BlockSpec((1, D_MODEL), const2),
                  pl.BlockSpec((D_MODEL, WIN_TOTAL), const2),
                  pl.BlockSpec((1, Q_LORA), const2),
                  pl.BlockSpec((Q_LORA, N_HEADS * QH_W), const2),
                  pl.BlockSpec((1, KV_LORA), const2),
                  pl.BlockSpec((N_HEADS, NOPE, KV_LORA), lambda i, j: (0, 0, 0)),
                  pl.BlockSpec((ts, ROPE), lambda i, j: (j, 0)),
                  pl.BlockSpec((ts, ROPE), lambda i, j: (j, 0))],
        out_specs=[pl.BlockSpec((1, ts, KV_IN), lambda i, j: (i, j, 0)),
                   pl.BlockSpec((1, N_HEADS, ts, KV_IN), lambda i, j: (i, 0, j, 0)),
                   pl.BlockSpec((1, ts, D_MODEL), lambda i, j: (i, j, 0)),
                   pl.BlockSpec((1, ts, 2 * D_MODEL), lambda i, j: (i, j, 0))],
        out_shape=[jax.ShapeDtypeStruct((b, s, KV_IN), BF16),
                   jax.ShapeDtypeStruct((b, N_HEADS, s, KV_IN), BF16),
                   jax.ShapeDtypeStruct((b, s, D_MODEL), BF16),
                   jax.ShapeDtypeStruct((b, s, 2 * D_MODEL), BF16)],
        compiler_params=pltpu.CompilerParams(vmem_limit_bytes=VMEM_LIMIT),
        name="proj",
    )(x, mod, g_pre, win_r, g_q, wuq_r, g_kv, wk_abs, cos_t, sin_t)


def _attn_kernel(q_ref, k_ref, wv_ref, o_ref, s_ref, p_ref):
    k = k_ref[0]
    v1 = jnp.concatenate([k[:, 0:KV_LORA], jnp.ones((k.shape[0], KV_LORA), BF16)], axis=1)

    def scores(hd):
        s_ref[hd % 2] = lax.dot_general(q_ref[0, hd], k, _NT, preferred_element_type=F32)

    def project(hd, o):
        a = jnp.dot(o, wv_ref[hd], preferred_element_type=F32)
        o_ref[0, :, hd * VHEAD:(hd + 1) * VHEAD] = a.astype(BF16)

    scores(0)
    pending = None
    for hd in range(N_HEADS):
        par = hd % 2
        if hd + 1 < N_HEADS:
            scores(hd + 1)
        s = s_ref[par]
        p_ref[par] = jnp.exp2(s - jnp.max(s, axis=-1, keepdims=True)).astype(BF16)
        ol = jnp.concatenate([jnp.dot(p_ref[par, r0:r0 + PV_ROWS, :], v1, preferred_element_type=F32)
                              for r0 in range(0, s.shape[0], PV_ROWS)], axis=0)
        o = ol[:, 0:KV_LORA] / ol[:, KV_LORA:KV_LORA + 1]
        if pending is not None:
            project(*pending)
        pending = (hd, o.astype(BF16))
    project(*pending)


def _attn_call(q, kall, wv):
    b, _, s, _ = q.shape
    lk = kall.shape[1]
    tq = min(TQ_ATTN, s)
    return pl.pallas_call(
        _attn_kernel,
        grid=(b, s // tq),
        in_specs=[pl.BlockSpec((1, N_HEADS, tq, KV_IN), lambda i, j: (i, 0, j, 0)),
                  pl.BlockSpec((1, lk, KV_IN), lambda i, j: (i, 0, 0)),
                  pl.BlockSpec((N_HEADS, KV_LORA, VHEAD), lambda i, j: (0, 0, 0))],
        out_specs=pl.BlockSpec((1, tq, D_MODEL), lambda i, j: (i, j, 0)),
        out_shape=jax.ShapeDtypeStruct((b, s, D_MODEL), BF16),
        scratch_shapes=[pltpu.VMEM((2, tq, lk), F32), pltpu.VMEM((2, tq, lk), BF16)],
        compiler_params=pltpu.CompilerParams(vmem_limit_bytes=VMEM_LIMIT),
        name="attn",
    )(q, kall, wv)


def _pool_kernel(xp_ref, w_ref, ps_ref, o_ref):
    g = pl.program_id(1)
    x = xp_ref[0].astype(F32)
    n = x.shape[0]
    t = lax.broadcasted_iota(jnp.int32, x.shape, 0)
    pad = POOL_MAX_HALF
    zeros = jnp.zeros((pad, x.shape[1]), F32)
    xe = jnp.concatenate([zeros, x, zeros], axis=0)
    ne = n + 2 * pad

    def down(a, k):
        return pltpu.roll(a, k, 0)

    def up(a, k):
        return pltpu.roll(a, ne - k, 0)

    w = down(xe, 1) + xe
    for lvl in range(1, POOL_GROUPS):
        k = 1 << (lvl - 1)
        w = jnp.where(g >= lvl, down(w, k) + up(w, k), w)
    w = w[pad:pad + n]
    half = lax.shift_left(jnp.int32(1), g)
    count = (jnp.minimum(t + half, n) - jnp.maximum(t - half, 0)).astype(F32)
    pooled = (w / count - x).astype(BF16)
    y = jnp.dot(pooled, w_ref[0], preferred_element_type=F32) * ps_ref[...]
    o_ref[0] = y.astype(BF16)


def _pool_call(pool_in, w_pool, pool_scale):
    b, s, _ = pool_in.shape
    gd = POOL_GROUP_DIM
    return pl.pallas_call(
        _pool_kernel,
        grid=(b, POOL_GROUPS),
        in_specs=[pl.BlockSpec((1, s, gd), lambda i, g: (i, 0, g)),
                  pl.BlockSpec((1, gd, gd), lambda i, g: (g, 0, 0)),
                  pl.BlockSpec((1, gd), lambda i, g: (0, g))],
        out_specs=pl.BlockSpec((1, s, gd), lambda i, g: (i, 0, g)),
        out_shape=jax.ShapeDtypeStruct((b, s, D_MODEL), BF16),
        compiler_params=pltpu.CompilerParams(vmem_limit_bytes=VMEM_LIMIT),
        name="pool",
    )(pool_in, w_pool, pool_scale)


SUB = 8


def _sort_network(n):
    pairs = []
    p = 1
    while p < PEER_TOPK:
        k = p
        while k >= 1:
            for j in range(k % p, PEER_TOPK - k, 2 * k):
                for i in range(min(k, PEER_TOPK - j - k)):
                    if (i + j) // (2 * p) == (i + j + k) // (2 * p):
                        pairs.append((i + j, i + j + k))
            k //= 2
        p *= 2
    return [(i, j) for i, j in pairs if j < n]


def _top16_sorted(vs, sort_last=True):
    vs = list(vs)
    for i, j in _sort_network(len(vs)):
        vs[i], vs[j] = jnp.maximum(vs[i], vs[j]), jnp.minimum(vs[i], vs[j])
    vs += [jnp.full(vs[0].shape, -jnp.inf, F32)] * (PEER_TOPK - len(vs))
    shift = SUB // 2
    while shift >= 1:
        other = [pltpu.roll(v, shift, 0) for v in vs]
        vs = [jnp.maximum(vs[k], other[PEER_TOPK - 1 - k]) for k in range(PEER_TOPK)]
        d = PEER_TOPK // 2 if (sort_last or shift > 1) else 0
        while d >= 1:
            for i in range(PEER_TOPK):
                if i & d == 0:
                    vs[i], vs[i + d] = jnp.maximum(vs[i], vs[i + d]), jnp.minimum(vs[i], vs[i + d])
            d //= 2
        shift //= 2
    return vs


def _peer_stats(s1, s2):
    lanes = s1.shape[1]
    n_v = N_KEYS // SUB
    rows = lax.broadcasted_iota(jnp.int32, (SUB, lanes), 0)
    v1 = [s1[SUB * k:SUB * (k + 1)] for k in range(n_v)]
    v2 = [s2[SUB * k:SUB * (k + 1)] for k in range(n_v)]
    a = _top16_sorted(v1)
    b = _top16_sorted(v2)

    def by_rank(vals):
        out = vals[0]
        for r in range(1, SUB):
            out = jnp.where(rows == r, vals[r], out)
        return out

    a_lo, a_hi, b_hi = by_rank(a[0:SUB]), by_rank(a[SUB:]), by_rank(b[SUB:])
    pieces = [a_lo + b[0], a_hi + b[0]] + [a_lo + b[c] for c in range(1, SUB)] + [b_hi + a[0]]
    thr = functools.reduce(jnp.minimum, _top16_sorted(pieces, sort_last=False))
    top = a[0] + b[0]
    sel = [p >= thr for p in pieces]
    z = sum(jnp.where(m, jnp.exp(p - top), 0.0) for m, p in zip(sel, pieces))
    inv_z = 1.0 / jnp.broadcast_to(jnp.sum(z, axis=0, keepdims=True), (SUB, lanes))
    one = [jnp.where(m, 1.0, 0.0) for m in sel]
    n_hi = one[1]
    n_lo = one[0] + sum(one[2:2 + SUB - 1])
    extra = jnp.broadcast_to(jnp.sum(one[-1], axis=0, keepdims=True), (SUB, lanes))
    n_lo = n_lo + jnp.where(rows == 0, extra, 0.0)
    n_of = [jnp.broadcast_to((n_lo if r < SUB else n_hi)[r % SUB:r % SUB + 1], (SUB, lanes))
            for r in range(PEER_TOPK)]
    cnt, e1, rank2, e2 = [], [], [], []
    for k in range(n_v):
        c = jnp.zeros((SUB, lanes), F32)
        rk = jnp.zeros((SUB, lanes), F32)
        for r in range(PEER_TOPK):
            c = jnp.where(v1[k] == a[r], n_of[r], c)
            rk = jnp.where(v2[k] < b[r], float(r + 1), rk)
        cnt.append(c)
        rank2.append(rk)
        e1.append(jnp.exp(v1[k] - a[0]) * inv_z)
        e2.append(jnp.exp(v2[k] - b[0]))
    cat = lambda vs: jnp.concatenate(vs, axis=0)
    return cat(cnt), cat(e1), cat(rank2), cat(e2)


def _merge_kernel(x_ref, attn_ref, yp_ref, gates_ref, mod_ref, wout_ref, gpost_ref, gffn_ref, wq_ref, keys_ref,
                  xmid_ref, hf_ref, rk_ref, e2_ref, cnt_ref, e1_ref):
    m = mod_ref[0]
    ga = gates_ref[0, :, 0:D_MODEL].astype(F32)
    gb = gates_ref[0, :, D_MODEL:2 * D_MODEL].astype(F32)
    merged = ga * attn_ref[0].astype(F32) + gb * yp_ref[0].astype(F32)
    y = jnp.dot(merged.astype(BF16), wout_ref[...], preferred_element_type=F32)
    x_mid = x_ref[0] + m[2:3] * (_rms(y) * gpost_ref[...])
    xmid_ref[0] = x_mid
    hf = (_rms(x_mid) * gffn_ref[...]) * (1.0 + m[4:5]) + m[3:4]
    hfb = hf.astype(BF16)
    hf_ref[0] = hfb
    q = jnp.dot(hfb, wq_ref[...], preferred_element_type=F32).astype(BF16)
    k1 = keys_ref[0]
    k2 = keys_ref[1]
    ts = q.shape[0]
    for hd in range(PEER_HEADS):
        base = hd * 2 * PEER_HALF
        s1 = lax.dot_general(k1, q[:, base:base + PEER_HALF], _NT, preferred_element_type=F32)
        s2 = lax.dot_general(k2, q[:, base + PEER_HALF:base + 2 * PEER_HALF], _NT, preferred_element_type=F32)
        for lt in range(ts // LANE):
            sl = slice(lt * LANE, (lt + 1) * LANE)
            blk, off = divmod(lt * LANE, PEER_LT)
            dst = slice(off, off + LANE)
            cnt, e1, rank2, e2 = _peer_stats(s1[:, sl], s2[:, sl])
            rk_ref[blk, hd, :, dst] = rank2.astype(BF16)
            e2_ref[blk, hd, :, dst] = e2.astype(BF16)
            cnt_ref[blk, hd, :, dst] = cnt
            e1_ref[blk, hd, :, dst] = e1


def _merge_call(x, attn, ypool, gates, mod, w_out, g_post, g_ffn, wq, keys):
    b, s, _ = x.shape
    ts = min(TS_MERGE, s)
    nt = s // ts
    n_lt = b * s // PEER_LT
    const2 = lambda i, j: (0, 0)
    tok3 = lambda i, j: (i, j, 0)
    lane_t = lambda i, j: (i * nt + j, 0, 0, 0)
    return pl.pallas_call(
        _merge_kernel,
        grid=(b, nt),
        in_specs=[pl.BlockSpec((1, ts, D_MODEL), tok3),
                  pl.BlockSpec((1, ts, D_MODEL), tok3),
                  pl.BlockSpec((1, ts, D_MODEL), tok3),
                  pl.BlockSpec((1, ts, 2 * D_MODEL), tok3),
                  pl.BlockSpec((1, N_MOD, D_MODEL), lambda i, j: (i, 0, 0)),
                  pl.BlockSpec((D_MODEL, D_MODEL), const2),
                  pl.BlockSpec((1, D_MODEL), const2),
                  pl.BlockSpec((1, D_MODEL), const2),
                  pl.BlockSpec((D_MODEL, PEER_HEADS * 2 * PEER_HALF), const2),
                  pl.BlockSpec((2, N_KEYS, PEER_HALF), lambda i, j: (0, 0, 0))],
        out_specs=[pl.BlockSpec((1, ts, D_MODEL), tok3),
                   pl.BlockSpec((1, ts, D_MODEL), tok3),
                   pl.BlockSpec((ts // PEER_LT, PEER_HEADS, N_KEYS, PEER_LT), lane_t),
                   pl.BlockSpec((ts // PEER_LT, PEER_HEADS, N_KEYS, PEER_LT), lane_t),
                   pl.BlockSpec((ts // PEER_LT, PEER_HEADS, N_KEYS, PEER_LT), lane_t),
                   pl.BlockSpec((ts // PEER_LT, PEER_HEADS, N_KEYS, PEER_LT), lane_t)],
        out_shape=[jax.ShapeDtypeStruct((b, s, D_MODEL), F32),
                   jax.ShapeDtypeStruct((b, s, D_MODEL), BF16),
                   jax.ShapeDtypeStruct((n_lt, PEER_HEADS, N_KEYS, PEER_LT), BF16),
                   jax.ShapeDtypeStruct((n_lt, PEER_HEADS, N_KEYS, PEER_LT), BF16),
                   jax.ShapeDtypeStruct((n_lt, PEER_HEADS, N_KEYS, PEER_LT), F32),
                   jax.ShapeDtypeStruct((n_lt, PEER_HEADS, N_KEYS, PEER_LT), F32)],
        compiler_params=pltpu.CompilerParams(vmem_limit_bytes=VMEM_LIMIT),
        name="merge",
    )(x, attn, ypool, gates, mod, w_out, g_post, g_ffn, wq, keys)


def _peer_kernel(hf_ref, u_ref, vt_ref, rk_ref, e2_ref, cnt_ref, e1_ref, xmid_ref, mod_ref, gpost_ref,
                 o_ref, acc_ref, act_ref, a_ref):
    e = pl.program_id(2)
    n_lt = hf_ref.shape[1] // PEER_LT

    @pl.when(e == 0)
    def _():
        acc_ref[...] = jnp.zeros_like(acc_ref)

    et = u_ref.shape[0]
    for lt in range(n_lt):
        hf_blk = hf_ref[0, lt * PEER_LT:(lt + 1) * PEER_LT, :]
        for r0 in range(0, et, PEER_CHUNK):
            act_ref[lt, r0:r0 + PEER_CHUNK, :] = lax.dot_general(
                u_ref[r0:r0 + PEER_CHUNK, :], hf_blk, _NT, preferred_element_type=F32)

    for lt in range(n_lt):
        cnt_hd = [cnt_ref[lt, hd].astype(BF16) for hd in range(PEER_HEADS)]
        e1_hd = [e1_ref[lt, hd].astype(BF16) for hd in range(PEER_HEADS)]
        for i in range(EI_PEER):
            w = jnp.zeros((N_KEYS, PEER_LT), BF16)
            for hd in range(PEER_HEADS):
                e2 = e2_ref[lt, hd]
                hit = rk_ref[lt, hd] < cnt_hd[hd][i:i + 1]
                w = w + jnp.where(hit, e2, jnp.zeros_like(e2)) * e1_hd[hd][i:i + 1]
            rows = slice(i * N_KEYS, (i + 1) * N_KEYS)
            a = act_ref[lt, rows, :].astype(BF16)
            gelu = (0.5 * a) * (1.0 + lax.erf(a * 0.7071067811865476))
            a_ref[lt, rows, :] = w * gelu

    for lt in range(n_lt):
        for r0 in range(0, et, PEER_CHUNK):
            acc_ref[lt] += jnp.dot(vt_ref[:, r0:r0 + PEER_CHUNK], a_ref[lt, r0:r0 + PEER_CHUNK, :],
                                   preferred_element_type=F32)

    @pl.when(e == pl.num_programs(2) - 1)
    def _():
        m = mod_ref[0]
        for lt in range(n_lt):
            tok = slice(lt * PEER_LT, (lt + 1) * PEER_LT)
            y = acc_ref[lt].T
            o_ref[0, tok, :] = xmid_ref[0, tok, :] + m[5:6] * (_rms(y) * gpost_ref[...])


def _peer_call(hf, u_b, vt_b, s2t, e2t, cutt, e1t, x_mid, mod, g_post):
    b, s, _ = hf.shape
    tt = min(TT_PEER, s)
    nt = s // tt
    et = EI_PEER * N_KEYS
    ne = N_KEYS // EI_PEER
    tok3 = lambda i, j, e: (i, j, 0)
    lane_t = lambda i, j, e: (i * nt + j, 0, 0, 0)
    return pl.pallas_call(
        _peer_kernel,
        grid=(b, nt, ne),
        in_specs=[pl.BlockSpec((1, tt, D_MODEL), tok3),
                  pl.BlockSpec((et, D_MODEL), lambda i, j, e: (e, 0)),
                  pl.BlockSpec((D_MODEL, et), lambda i, j, e: (0, e)),
                  pl.BlockSpec((tt // PEER_LT, PEER_HEADS, N_KEYS, PEER_LT), lane_t),
                  pl.BlockSpec((tt // PEER_LT, PEER_HEADS, N_KEYS, PEER_LT), lane_t),
                  pl.BlockSpec((tt // PEER_LT, PEER_HEADS, EI_PEER, PEER_LT), lambda i, j, e: (i * nt + j, 0, e, 0)),
                  pl.BlockSpec((tt // PEER_LT, PEER_HEADS, EI_PEER, PEER_LT), lambda i, j, e: (i * nt + j, 0, e, 0)),
                  pl.BlockSpec((1, tt, D_MODEL), tok3),
                  pl.BlockSpec((1, N_MOD, D_MODEL), lambda i, j, e: (i, 0, 0)),
                  pl.BlockSpec((1, D_MODEL), lambda i, j, e: (0, 0))],
        out_specs=pl.BlockSpec((1, tt, D_MODEL), tok3),
        out_shape=jax.ShapeDtypeStruct((b, s, D_MODEL), F32),
        scratch_shapes=[pltpu.VMEM((tt // PEER_LT, D_MODEL, PEER_LT), F32),
                        pltpu.VMEM((tt // PEER_LT, et, PEER_LT), F32),
                        pltpu.VMEM((tt // PEER_LT, et, PEER_LT), BF16)],
        compiler_params=pltpu.CompilerParams(
            vmem_limit_bytes=VMEM_LIMIT,
            dimension_semantics=("arbitrary", "arbitrary", "arbitrary")),
        name="peer",
    )(hf, u_b, vt_b, s2t, e2t, cutt, e1t, x_mid, mod, g_post)


def _rope_tables(seq):
    rows = seq // GRID_W
    row = jnp.broadcast_to(jnp.arange(rows, dtype=F32)[:, None], (rows, GRID_W)).reshape(-1)
    col = jnp.broadcast_to(jnp.arange(GRID_W, dtype=F32)[None, :], (rows, GRID_W)).reshape(-1)
    n_freq = ROPE // 4
    inv_freq = ROPE_BASE ** (-jnp.arange(n_freq, dtype=F32) / n_freq)
    ar = row[:, None] * inv_freq
    ac = col[:, None] * inv_freq
    cos = jnp.concatenate([jnp.cos(ar), jnp.cos(ar), jnp.cos(ac), jnp.cos(ac)], axis=1)
    sin = jnp.concatenate([-jnp.sin(ar), jnp.sin(ar), -jnp.sin(ac), jnp.sin(ac)], axis=1)
    return cos, sin


def _swap_halves(w):
    q = ROPE // 4
    return jnp.concatenate([w[..., q:2 * q], w[..., 0:q], w[..., 3 * q:4 * q], w[..., 2 * q:3 * q]], axis=-1)


def kernel(x, c, ctx, c_ctx, w_ada, b_ada, g_mix_pre, g_mix_post, g_ffn_pre, g_ffn_post, w_in, g_q_lora, w_uq,
           g_kv_lora, w_ukv, w_pool, pool_scale, w_out, peer_wq, peer_keys, peer_u, peer_v):
    assert w_ada.shape[0] == 1, "single-layer block"
    b, s, _ = x.shape
    layer = 0

    w_in_l = w_in[layer]
    k_rope_w = w_in_l[:, KV_LORA:KV_IN]
    win_r = jnp.concatenate([w_in_l[:, 0:KV_IN], _swap_halves(k_rope_w), w_in_l[:, KV_IN:]], axis=1).astype(BF16)
    wuq_h = w_uq[layer].reshape(Q_LORA, N_HEADS, QK_HEAD)
    wuq_r = jnp.concatenate([wuq_h, _swap_halves(wuq_h[..., NOPE:])], axis=-1).reshape(Q_LORA, N_HEADS * QH_W)
    wuq_r = wuq_r.astype(BF16)
    wukv_h = w_ukv[layer].reshape(KV_LORA, N_HEADS, NOPE + VHEAD)
    wk_abs = jnp.transpose(wukv_h[..., 0:NOPE], (1, 2, 0)).astype(BF16)
    wv_up = jnp.transpose(wukv_h[..., NOPE:], (1, 0, 2)).astype(BF16)
    cos_t, sin_t = _rope_tables(s)
    u_b = peer_u[layer].astype(BF16)
    vt_b = peer_v[layer].T.astype(BF16)
    row = lambda v: v[layer][None, :]

    n_rows = -(-(b + 1) // 8) * 8
    cc = jnp.concatenate([c, c_ctx[None, :], jnp.zeros((n_rows - b - 1, D_MODEL), F32)], axis=0)
    mod_all = _mod_call(cc, w_ada[layer], b_ada[layer][None, :])
    mod = mod_all[:b].reshape(b, N_MOD, D_MODEL)
    mod_c = mod_all[b].reshape(N_MOD, D_MODEL)

    k_ctx = _ctx_call(ctx, mod_c, row(g_mix_pre), win_r[:, 0:WIN_CQ], row(g_kv_lora))
    k_lat, q, pool_in, gates = _proj_call(x, mod, row(g_mix_pre), win_r, row(g_q_lora), wuq_r, row(g_kv_lora),
                                          wk_abs, cos_t, sin_t)
    k_all = jnp.concatenate([k_ctx, k_lat], axis=1)
    attn = _attn_call(q, k_all, wv_up)
    ypool = _pool_call(pool_in, w_pool[layer].astype(BF16), row(pool_scale))
    x_mid, hf, s2t, e2t, cutt, e1t = _merge_call(
        x, attn, ypool, gates, mod, w_out[layer].astype(BF16), row(g_mix_post), row(g_ffn_pre),
        peer_wq[layer].astype(BF16), peer_keys[layer].astype(BF16))
    return _peer_call(hf, u_b, vt_b, s2t, e2t, cutt, e1t, x_mid, mod, row(g_ffn_post))
```

```python
import functools

import jax
import jax.numpy as jnp
from jax import lax
from jax.experimental import pallas as pl
from jax.experimental.pallas import tpu as pltpu

F32 = jnp.float32
BF16 = jnp.bfloat16

D_MODEL = 1024
GRID_W = 64
N_HEADS = 8
NOPE = 128
ROPE = 64
VHEAD = 128
QK_HEAD = NOPE + ROPE
Q_LORA = 256
KV_LORA = 128
ROPE_BASE = 10000.0
ATTN_SCALE = QK_HEAD ** -0.5
Q_SCALE = ATTN_SCALE * 1.4426950408889634
POOL_GROUPS = 4
POOL_GROUP_DIM = D_MODEL // POOL_GROUPS
POOL_MAX_HALF = 1 << (POOL_GROUPS - 1)
KV_IN = KV_LORA + ROPE
PEER_HEADS = 8
N_KEYS = 128
PEER_HALF = 128
PEER_TOPK = 16
EPS = 1e-6

WIN_CQ = KV_IN + ROPE
WIN_SMALL = WIN_CQ + Q_LORA
N_MOD = 6
WIN_POOL = WIN_SMALL
WIN_GATE = WIN_POOL + D_MODEL
WIN_TOTAL = WIN_GATE + 2 * D_MODEL
QH_W = 256

LANE = 128
PEER_LT = 256
VMEM_LIMIT = 56 * 1024 * 1024

TS_PROJ = 512
TQ_ATTN = 1024
PV_ROWS = 128
TS_MERGE = 256
TT_PEER = 1024
EI_PEER = 8
PEER_CHUNK = 256

_NT = (((1,), (1,)), ((), ()))


def _rms(x):
    return x * lax.rsqrt(jnp.mean(x * x, axis=-1, keepdims=True) + EPS)


def _mod_kernel(cc_ref, w_ref, b_ref, o_ref):
    cc = cc_ref[...]
    a = cc * jax.nn.sigmoid(cc)
    o_ref[...] = jnp.dot(a, w_ref[...], preferred_element_type=F32) + b_ref[...]


def _mod_call(cc, w_ada, b_ada):
    rows = cc.shape[0]
    n = w_ada.shape[1]
    bn = 1024
    return pl.pallas_call(
        _mod_kernel,
        grid=(n // bn,),
        in_specs=[pl.BlockSpec((rows, D_MODEL), lambda j: (0, 0)),
                  pl.BlockSpec((D_MODEL, bn), lambda j: (0, j)),
                  pl.BlockSpec((1, bn), lambda j: (0, j))],
        out_specs=pl.BlockSpec((rows, bn), lambda j: (0, j)),
        out_shape=jax.ShapeDtypeStruct((rows, n), F32),
        compiler_params=pltpu.CompilerParams(vmem_limit_bytes=VMEM_LIMIT),
        name="mod",
    )(cc, w_ada, b_ada)


def _ctx_kernel(ctx_ref, mod_ref, gpre_ref, win_ref, gkv_ref, o_ref):
    x = ctx_ref[0]
    m = mod_ref[...]
    h = _rms(x) * gpre_ref[...]
    h = h * (1.0 + m[1:2]) + m[0:1]
    p = jnp.dot(h.astype(BF16), win_ref[...], preferred_element_type=F32)
    ckv_n = _rms(p[:, :KV_LORA]) * gkv_ref[...]
    o_ref[0, :, 0:KV_LORA] = ckv_n.astype(BF16)
    o_ref[0, :, KV_LORA:KV_IN] = p[:, KV_LORA:KV_IN].astype(BF16)


def _ctx_call(ctx, mod_c, g_pre, win_r, g_kv):
    b, lc, _ = ctx.shape
    return pl.pallas_call(
        _ctx_kernel,
        grid=(b,),
        in_specs=[pl.BlockSpec((1, lc, D_MODEL), lambda i: (i, 0, 0)),
                  pl.BlockSpec((N_MOD, D_MODEL), lambda i: (0, 0)),
                  pl.BlockSpec((1, D_MODEL), lambda i: (0, 0)),
                  pl.BlockSpec((D_MODEL, WIN_CQ), lambda i: (0, 0)),
                  pl.BlockSpec((1, KV_LORA), lambda i: (0, 0))],
        out_specs=pl.BlockSpec((1, lc, KV_IN), lambda i: (i, 0, 0)),
        out_shape=jax.ShapeDtypeStruct((b, lc, KV_IN), BF16),
        compiler_params=pltpu.CompilerParams(vmem_limit_bytes=VMEM_LIMIT),
        name="ctx_kv",
    )(ctx, mod_c, g_pre, win_r, g_kv)


def _proj_kernel(x_ref, mod_ref, gpre_ref, win_ref, gq_ref, wuq_ref, gkv_ref, wk_ref, cos_ref, sin_ref,
                 kcat_ref, q_ref, pool_ref, gates_ref):
    x = x_ref[0]
    m = mod_ref[0]
    h = _rms(x) * gpre_ref[...]
    hb = (h * (1.0 + m[1:2]) + m[0:1]).astype(BF16)
    cos = cos_ref[...]
    sin = sin_ref[...]

    p0 = jnp.dot(hb, win_ref[:, 0:WIN_SMALL], preferred_element_type=F32)
    ckv_n = _rms(p0[:, 0:KV_LORA]) * gkv_ref[...]
    k_rot = p0[:, KV_LORA:KV_IN] * cos + p0[:, KV_IN:WIN_CQ] * sin
    kcat_ref[0, :, 0:KV_LORA] = ckv_n.astype(BF16)
    kcat_ref[0, :, KV_LORA:KV_IN] = k_rot.astype(BF16)

    cqn = (_rms(p0[:, WIN_CQ:WIN_SMALL]) * gq_ref[...]).astype(BF16)

    pool_ref[0] = jnp.dot(hb, win_ref[:, WIN_POOL:WIN_GATE], preferred_element_type=F32).astype(BF16)
    qh = [jnp.dot(cqn, wuq_ref[:, hd * QH_W:(hd + 1) * QH_W], preferred_element_type=F32)
          for hd in range(N_HEADS)]
    for half in range(2):
        lo = WIN_GATE + half * D_MODEL
        g = jnp.dot(hb, win_ref[:, lo:lo + D_MODEL], preferred_element_type=F32)
        gates_ref[0, :, half * D_MODEL:(half + 1) * D_MODEL] = jax.nn.sigmoid(g).astype(BF16)
    for hd in range(N_HEADS):
        q_abs = jnp.dot(qh[hd][:, 0:NOPE].astype(BF16), wk_ref[hd], preferred_element_type=F32)
        q_rot = qh[hd][:, NOPE:QK_HEAD] * cos + qh[hd][:, QK_HEAD:QH_W] * sin
        q_ref[0, hd, :, 0:KV_LORA] = (q_abs * Q_SCALE).astype(BF16)
        q_ref[0, hd, :, KV_LORA:KV_IN] = (q_rot * Q_SCALE).astype(BF16)


def _proj_call(x, mod, g_pre, win_r, g_q, wuq_r, g_kv, wk_abs, cos_t, sin_t):
    b, s, _ = x.shape
    ts = min(TS_PROJ, s)
    const2 = lambda i, j: (0, 0)
    return pl.pallas_call(
        _proj_kernel,
        grid=(b, s // ts),
        in_specs=[pl.BlockSpec((1, ts, D_MODEL), lambda i, j: (i, j, 0)),
                  pl.BlockSpec((1, N_MOD, D_MODEL), lambda i, j: (i, 0, 0)),
                  pl.BlockSpec((1, D_MODEL), const2),
                  pl.BlockSpec((D_MODEL, WIN_TOTAL), const2),
                  pl.BlockSpec((1, Q_LORA), const2),
                  pl.BlockSpec((Q_LORA, N_HEADS * QH_W), const2),
                  pl.BlockSpec((1, KV_LORA), const2),
                  pl.BlockSpec((N_HEADS, NOPE, KV_LORA), lambda i, j: (0, 0, 0)),
                  pl.BlockSpec((ts, ROPE), lambda i, j: (j, 0)),
                  pl.BlockSpec((ts, ROPE), lambda i, j: (j, 0))],
        out_specs=[pl.BlockSpec((1, ts, KV_IN), lambda i, j: (i, j, 0)),
                   pl.BlockSpec((1, N_HEADS, ts, KV_IN), lambda i, j: (i, 0, j, 0)),
                   pl.BlockSpec((1, ts, D_MODEL), lambda i, j: (i, j, 0)),
                   pl.BlockSpec((1, ts, 2 * D_MODEL), lambda i, j: (i, j, 0))],
        out_shape=[jax.ShapeDtypeStruct((b, s, KV_IN), BF16),
                   jax.ShapeDtypeStruct((b, N_HEADS, s, KV_IN), BF16),
                   jax.ShapeDtypeStruct((b, s, D_MODEL), BF16),
                   jax.ShapeDtypeStruct((b, s, 2 * D_MODEL), BF16)],
        compiler_params=pltpu.CompilerParams(vmem_limit_bytes=VMEM_LIMIT),
        name="proj",
    )(x, mod, g_pre, win_r, g_q, wuq_r, g_kv, wk_abs, cos_t, sin_t)


def _attn_kernel(q_ref, k_ref, wv_ref, o_ref, s_ref, p_ref):
    k = k_ref[0]
    v1 = jnp.concatenate([k[:, 0:KV_LORA], jnp.ones((k.shape[0], KV_LORA), BF16)], axis=1)

    def scores(hd):
        s_ref[hd % 2] = lax.dot_general(q_ref[0, hd], k, _NT, preferred_element_type=F32)

    def project(hd, o):
        a = jnp.dot(o, wv_ref[hd], preferred_element_type=F32)
        o_ref[0, :, hd * VHEAD:(hd + 1) * VHEAD] = a.astype(BF16)

    scores(0)
    pending = None
    for hd in range(N_HEADS):
        par = hd % 2
        if hd + 1 < N_HEADS:
            scores(hd + 1)
        s = s_ref[par]
        p_ref[par] = jnp.exp2(s - jnp.max(s, axis=-1, keepdims=True)).astype(BF16)
        ol = jnp.concatenate([jnp.dot(p_ref[par, r0:r0 + PV_ROWS, :], v1, preferred_element_type=F32)
                              for r0 in range(0, s.shape[0], PV_ROWS)], axis=0)
        o = ol[:, 0:KV_LORA] / ol[:, KV_LORA:KV_LORA + 1]
        if pending is not None:
            project(*pending)
        pending = (hd, o.astype(BF16))
    project(*pending)


def _attn_call(q, kall, wv):
    b, _, s, _ = q.shape
    lk = kall.shape[1]
    tq = min(TQ_ATTN, s)
    return pl.pallas_call(
        _attn_kernel,
        grid=(b, s // tq),
        in_specs=[pl.BlockSpec((1, N_HEADS, tq, KV_IN), lambda i, j: (i, 0, j, 0)),
                  pl.BlockSpec((1, lk, KV_IN), lambda i, j: (i, 0, 0)),
                  pl.BlockSpec((N_HEADS, KV_LORA, VHEAD), lambda i, j: (0, 0, 0))],
        out_specs=pl.BlockSpec((1, tq, D_MODEL), lambda i, j: (i, j, 0)),
        out_shape=jax.ShapeDtypeStruct((b, s, D_MODEL), BF16),
        scratch_shapes=[pltpu.VMEM((2, tq, lk), F32), pltpu.VMEM((2, tq, lk), BF16)],
        compiler_params=pltpu.CompilerParams(vmem_limit_bytes=VMEM_LIMIT),
        name="attn",
    )(q, kall, wv)


def _pool_kernel(xp_ref, w_ref, ps_ref, o_ref):
    g = pl.program_id(1)
    x = xp_ref[0].astype(F32)
    n = x.shape[0]
    t = lax.broadcasted_iota(jnp.int32, x.shape, 0)
    pad = POOL_MAX_HALF
    zeros = jnp.zeros((pad, x.shape[1]), F32)
    xe = jnp.concatenate([zeros, x, zeros], axis=0)
    ne = n + 2 * pad

    def down(a, k):
        return pltpu.roll(a, k, 0)

    def up(a, k):
        return pltpu.roll(a, ne - k, 0)

    w = down(xe, 1) + xe
    for lvl in range(1, POOL_GROUPS):
        k = 1 << (lvl - 1)
        w = jnp.where(g >= lvl, down(w, k) + up(w, k), w)
    w = w[pad:pad + n]
    half = lax.shift_left(jnp.int32(1), g)
    count = (jnp.minimum(t + half, n) - jnp.maximum(t - half, 0)).astype(F32)
    pooled = (w / count - x).astype(BF16)
    y = jnp.dot(pooled, w_ref[0], preferred_element_type=F32) * ps_ref[...]
    o_ref[0] = y.astype(BF16)


def _pool_call(pool_in, w_pool, pool_scale):
    b, s, _ = pool_in.shape
    gd = POOL_GROUP_DIM
    return pl.pallas_call(
        _pool_kernel,
        grid=(b, POOL_GROUPS),
        in_specs=[pl.BlockSpec((1, s, gd), lambda i, g: (i, 0, g)),
                  pl.BlockSpec((1, gd, gd), lambda i, g: (g, 0, 0)),
                  pl.BlockSpec((1, gd), lambda i, g: (0, g))],
        out_specs=pl.BlockSpec((1, s, gd), lambda i, g: (i, 0, g)),
        out_shape=jax.ShapeDtypeStruct((b, s, D_MODEL), BF16),
        compiler_params=pltpu.CompilerParams(vmem_limit_bytes=VMEM_LIMIT),
        name="pool",
    )(pool_in, w_pool, pool_scale)


SUB = 8


def _sort_network(n):
    pairs = []
    p = 1
    while p < PEER_TOPK:
        k = p
        while k >= 1:
            for j in range(k % p, PEER_TOPK - k, 2 * k):
                for i in range(min(k, PEER_TOPK - j - k)):
                    if (i + j) // (2 * p) == (i + j + k) // (2 * p):
                        pairs.append((i + j, i + j + k))
            k //= 2
        p *= 2
    return [(i, j) for i, j in pairs if j < n]


def _top16_sorted(vs, sort_last=True):
    vs = list(vs)
    for i, j in _sort_network(len(vs)):
        vs[i], vs[j] = jnp.maximum(vs[i], vs[j]), jnp.minimum(vs[i], vs[j])
    vs += [jnp.full(vs[0].shape, -jnp.inf, F32)] * (PEER_TOPK - len(vs))
    shift = SUB // 2
    while shift >= 1:
        other = [pltpu.roll(v, shift, 0) for v in vs]
        vs = [jnp.maximum(vs[k], other[PEER_TOPK - 1 - k]) for k in range(PEER_TOPK)]
        d = PEER_TOPK // 2 if (sort_last or shift > 1) else 0
        while d >= 1:
            for i in range(PEER_TOPK):
                if i & d == 0:
                    vs[i], vs[i + d] = jnp.maximum(vs[i], vs[i + d]), jnp.minimum(vs[i], vs[i + d])
            d //= 2
        shift //= 2
    return vs


def _peer_stats(s1, s2):
    lanes = s1.shape[1]
    n_v = N_KEYS // SUB
    rows = lax.broadcasted_iota(jnp.int32, (SUB, lanes), 0)
    v1 = [s1[SUB * k:SUB * (k + 1)] for k in range(n_v)]
    v2 = [s2[SUB * k:SUB * (k + 1)] for k in range(n_v)]
    a = _top16_sorted(v1)
    b = _top16_sorted(v2)

    def by_rank(vals):
        out = vals[0]
        for r in range(1, SUB):
            out = jnp.where(rows == r, vals[r], out)
        return out

    a_lo, a_hi, b_hi = by_rank(a[0:SUB]), by_rank(a[SUB:]), by_rank(b[SUB:])
    pieces = [a_lo + b[0], a_hi + b[0]] + [a_lo + b[c] for c in range(1, SUB)] + [b_hi + a[0]]
    thr = functools.reduce(jnp.minimum, _top16_sorted(pieces, sort_last=False))
    top = a[0] + b[0]
    sel = [p >= thr for p in pieces]
    z = sum(jnp.where(m, jnp.exp(p - top), 0.0) for m, p in zip(sel, pieces))
    inv_z = 1.0 / jnp.broadcast_to(jnp.sum(z, axis=0, keepdims=True), (SUB, lanes))
    one = [jnp.where(m, 1.0, 0.0) for m in sel]
    n_hi = one[1]
    n_lo = one[0] + sum(one[2:2 + SUB - 1])
    extra = jnp.broadcast_to(jnp.sum(one[-1], axis=0, keepdims=True), (SUB, lanes))
    n_lo = n_lo + jnp.where(rows == 0, extra, 0.0)
    n_of = [jnp.broadcast_to((n_lo if r < SUB else n_hi)[r % SUB:r % SUB + 1], (SUB, lanes))
            for r in range(PEER_TOPK)]
    cnt, e1, rank2, e2 = [], [], [], []
    for k in range(n_v):
        c = jnp.zeros((SUB, lanes), F32)
        rk = jnp.zeros((SUB, lanes), F32)
        for r in range(PEER_TOPK):
            c = jnp.where(v1[k] == a[r], n_of[r], c)
            rk = jnp.where(v2[k] < b[r], float(r + 1), rk)
        cnt.append(c)
        rank2.append(rk)
        e1.append(jnp.exp(v1[k] - a[0]) * inv_z)
        e2.append(jnp.exp(v2[k] - b[0]))
    cat = lambda vs: jnp.concatenate(vs, axis=0)
    return cat(cnt), cat(e1), cat(rank2), cat(e2)


def _merge_kernel(x_ref, attn_ref, yp_ref, gates_ref, mod_ref, wout_ref, gpost_ref, gffn_ref, wq_ref, keys_ref,
                  xmid_ref, hf_ref, rk_ref, e2_ref, cnt_ref, e1_ref):
    m = mod_ref[0]
    ga = gates_ref[0, :, 0:D_MODEL].astype(F32)
    gb = gates_ref[0, :, D_MODEL:2 * D_MODEL].astype(F32)
    merged = ga * attn_ref[0].astype(F32) + gb * yp_ref[0].astype(F32)
    y = jnp.dot(merged.astype(BF16), wout_ref[...], preferred_element_type=F32)
    x_mid = x_ref[0] + m[2:3] * (_rms(y) * gpost_ref[...])
    xmid_ref[0] = x_mid
    hf = (_rms(x_mid) * gffn_ref[...]) * (1.0 + m[4:5]) + m[3:4]
    hfb = hf.astype(BF16)
    hf_ref[0] = hfb
    q = jnp.dot(hfb, wq_ref[...], preferred_element_type=F32).astype(BF16)
    k1 = keys_ref[0]
    k2 = keys_ref[1]
    ts = q.shape[0]
    for hd in range(PEER_HEADS):
        base = hd * 2 * PEER_HALF
        s1 = lax.dot_general(k1, q[:, base:base + PEER_HALF], _NT, preferred_element_type=F32)
        s2 = lax.dot_general(k2, q[:, base + PEER_HALF:base + 2 * PEER_HALF], _NT, preferred_element_type=F32)
        for lt in range(ts // LANE):
            sl = slice(lt * LANE, (lt + 1) * LANE)
            blk, off = divmod(lt * LANE, PEER_LT)
            dst = slice(off, off + LANE)
            cnt, e1, rank2, e2 = _peer_stats(s1[:, sl], s2[:, sl])
            rk_ref[blk, hd, :, dst] = rank2.astype(BF16)
            e2_ref[blk, hd, :, dst] = e2.astype(BF16)
            cnt_ref[blk, hd, :, dst] = cnt
            e1_ref[blk, hd, :, dst] = e1


def _merge_call(x, attn, ypool, gates, mod, w_out, g_post, g_ffn, wq, keys):
    b, s, _ = x.shape
    ts = min(TS_MERGE, s)
    nt = s // ts
    n_lt = b * s // PEER_LT
    const2 = lambda i, j: (0, 0)
    tok3 = lambda i, j: (i, j, 0)
    lane_t = lambda i, j: (i * nt + j, 0, 0, 0)
    return pl.pallas_call(
        _merge_kernel,
        grid=(b, nt),
        in_specs=[pl.BlockSpec((1, ts, D_MODEL), tok3),
                  pl.BlockSpec((1, ts, D_MODEL), tok3),
                  pl.BlockSpec((1, ts, D_MODEL), tok3),
                  pl.BlockSpec((1, ts, 2 * D_MODEL), tok3),
                  pl.BlockSpec((1, N_MOD, D_MODEL), lambda i, j: (i, 0, 0)),
                  pl.BlockSpec((D_MODEL, D_MODEL), const2),
                  pl.BlockSpec((1, D_MODEL), const2),
                  pl.BlockSpec((1, D_MODEL), const2),
                  pl.BlockSpec((D_MODEL, PEER_HEADS * 2 * PEER_HALF), const2),
                  pl.BlockSpec((2, N_KEYS, PEER_HALF), lambda i, j: (0, 0, 0))],
        out_specs=[pl.BlockSpec((1, ts, D_MODEL), tok3),
                   pl.BlockSpec((1, ts, D_MODEL), tok3),
                   pl.BlockSpec((ts // PEER_LT, PEER_HEADS, N_KEYS, PEER_LT), lane_t),
                   pl.BlockSpec((ts // PEER_LT, PEER_HEADS, N_KEYS, PEER_LT), lane_t),
                   pl.BlockSpec((ts // PEER_LT, PEER_HEADS, N_KEYS, PEER_LT), lane_t),
                   pl.BlockSpec((ts // PEER_LT, PEER_HEADS, N_KEYS, PEER_LT), lane_t)],
        out_shape=[jax.ShapeDtypeStruct((b, s, D_MODEL), F32),
                   jax.ShapeDtypeStruct((b, s, D_MODEL), BF16),
                   jax.ShapeDtypeStruct((n_lt, PEER_HEADS, N_KEYS, PEER_LT), BF16),
                   jax.ShapeDtypeStruct((n_lt, PEER_HEADS, N_KEYS, PEER_LT), BF16),
                   jax.ShapeDtypeStruct((n_lt, PEER_HEADS, N_KEYS, PEER_LT), F32),
                   jax.ShapeDtypeStruct((n_lt, PEER_HEADS, N_KEYS, PEER_LT), F32)],
        compiler_params=pltpu.CompilerParams(vmem_limit_bytes=VMEM_LIMIT),
        name="merge",
    )(x, attn, ypool, gates, mod, w_out, g_post, g_ffn, wq, keys)


def _peer_kernel(hf_ref, u_ref, vt_ref, rk_ref, e2_ref, cnt_ref, e1_ref, xmid_ref, mod_ref, gpost_ref,
                 o_ref, acc_ref, act_ref, a_ref):
    e = pl.program_id(2)
    n_lt = hf_ref.shape[1] // PEER_LT

    @pl.when(e == 0)
    def _():
        acc_ref[...] = jnp.zeros_like(acc_ref)

    et = u_ref.shape[0]
    for lt in range(n_lt):
        hf_blk = hf_ref[0, lt * PEER_LT:(lt + 1) * PEER_LT, :]
        for r0 in range(0, et, PEER_CHUNK):
            act_ref[lt, r0:r0 + PEER_CHUNK, :] = lax.dot_general(
                u_ref[r0:r0 + PEER_CHUNK, :], hf_blk, _NT, preferred_element_type=F32)

    for lt in range(n_lt):
        cnt_hd = [cnt_ref[lt, hd].astype(BF16) for hd in range(PEER_HEADS)]
        e1_hd = [e1_ref[lt, hd].astype(BF16) for hd in range(PEER_HEADS)]
        for i in range(EI_PEER):
            w = jnp.zeros((N_KEYS, PEER_LT), BF16)
            for hd in range(PEER_HEADS):
                e2 = e2_ref[lt, hd]
                hit = rk_ref[lt, hd] < cnt_hd[hd][i:i + 1]
                w = w + jnp.where(hit, e2, jnp.zeros_like(e2)) * e1_hd[hd][i:i + 1]
            rows = slice(i * N_KEYS, (i + 1) * N_KEYS)
            a = act_ref[lt, rows, :].astype(BF16)
            gelu = (0.5 * a) * (1.0 + lax.erf(a * 0.7071067811865476))
            a_ref[lt, rows, :] = w * gelu

    for lt in range(n_lt):
        for r0 in range(0, et, PEER_CHUNK):
            acc_ref[lt] += jnp.dot(vt_ref[:, r0:r0 + PEER_CHUNK], a_ref[lt, r0:r0 + PEER_CHUNK, :],
                                   preferred_element_type=F32)

    @pl.when(e == pl.num_programs(2) - 1)
    def _():
        m = mod_ref[0]
        for lt in range(n_lt):
            tok = slice(lt * PEER_LT, (lt + 1) * PEER_LT)
            y = acc_ref[lt].T
            o_ref[0, tok, :] = xmid_ref[0, tok, :] + m[5:6] * (_rms(y) * gpost_ref[...])


def _peer_call(hf, u_b, vt_b, s2t, e2t, cutt, e1t, x_mid, mod, g_post):
    b, s, _ = hf.shape
    tt = min(TT_PEER, s)
    nt = s // tt
    et = EI_PEER * N_KEYS
    ne = N_KEYS // EI_PEER
    tok3 = lambda i, j, e: (i, j, 0)
    lane_t = lambda i, j, e: (i * nt + j, 0, 0, 0)
    return pl.pallas_call(
        _peer_kernel,
        grid=(b, nt, ne),
        in_specs=[pl.BlockSpec((1, tt, D_MODEL), tok3),
                  pl.BlockSpec((et, D_MODEL), lambda i, j, e: (e, 0)),
                  pl.BlockSpec((D_MODEL, et), lambda i, j, e: (0, e)),
                  pl.BlockSpec((tt // PEER_LT, PEER_HEADS, N_KEYS, PEER_LT), lane_t),
                  pl.BlockSpec((tt // PEER_LT, PEER_HEADS, N_KEYS, PEER_LT), lane_t),
                  pl.BlockSpec((tt // PEER_LT, PEER_HEADS, EI_PEER, PEER_LT), lambda i, j, e: (i * nt + j, 0, e, 0)),
                  pl.BlockSpec((tt // PEER_LT, PEER_HEADS, EI_PEER, PEER_LT), lambda i, j, e: (i * nt + j, 0, e, 0)),
                  pl.BlockSpec((1, tt, D_MODEL), tok3),
                  pl.BlockSpec((1, N_MOD, D_MODEL), lambda i, j, e: (i, 0, 0)),
                  pl.BlockSpec((1, D_MODEL), lambda i, j, e: (0, 0))],
        out_specs=pl.BlockSpec((1, tt, D_MODEL), tok3),
        out_shape=jax.ShapeDtypeStruct((b, s, D_MODEL), F32),
        scratch_shapes=[pltpu.VMEM((tt // PEER_LT, D_MODEL, PEER_LT), F32),
                        pltpu.VMEM((tt // PEER_LT, et, PEER_LT), F32),
                        pltpu.VMEM((tt // PEER_LT, et, PEER_LT), BF16)],
        compiler_params=pltpu.CompilerParams(
            vmem_limit_bytes=VMEM_LIMIT,
            dimension_semantics=("arbitrary", "arbitrary", "arbitrary")),
        name="peer",
    )(hf, u_b, vt_b, s2t, e2t, cutt, e1t, x_mid, mod, g_post)


def _rope_tables(seq):
    rows = seq // GRID_W
    row = jnp.broadcast_to(jnp.arange(rows, dtype=F32)[:, None], (rows, GRID_W)).reshape(-1)
    col = jnp.broadcast_to(jnp.arange(GRID_W, dtype=F32)[None, :], (rows, GRID_W)).reshape(-1)
    n_freq = ROPE // 4
    inv_freq = ROPE_BASE ** (-jnp.arange(n_freq, dtype=F32) / n_freq)
    ar = row[:, None] * inv_freq
    ac = col[:, None] * inv_freq
    cos = jnp.concatenate([jnp.cos(ar), jnp.cos(ar), jnp.cos(ac), jnp.cos(ac)], axis=1)
    sin = jnp.concatenate([-jnp.sin(ar), jnp.sin(ar), -jnp.sin(ac), jnp.sin(ac)], axis=1)
    return cos, sin


def _swap_halves(w):
    q = ROPE // 4
    return jnp.concatenate([w[..., q:2 * q], w[..., 0:q], w[..., 3 * q:4 * q], w[..., 2 * q:3 * q]], axis=-1)


def kernel(x, c, ctx, c_ctx, w_ada, b_ada, g_mix_pre, g_mix_post, g_ffn_pre, g_ffn_post, w_in, g_q_lora, w_uq,
           g_kv_lora, w_ukv, w_pool, pool_scale, w_out, peer_wq, peer_keys, peer_u, peer_v):
    assert w_ada.shape[0] == 1, "single-layer block"
    b, s, _ = x.shape
    layer = 0

    w_in_l = w_in[layer]
    k_rope_w = w_in_l[:, KV_LORA:KV_IN]
    win_r = jnp.concatenate([w_in_l[:, 0:KV_IN], _swap_halves(k_rope_w), w_in_l[:, KV_IN:]], axis=1).astype(BF16)
    wuq_h = w_uq[layer].reshape(Q_LORA, N_HEADS, QK_HEAD)
    wuq_r = jnp.concatenate([wuq_h, _swap_halves(wuq_h[..., NOPE:])], axis=-1).reshape(Q_LORA, N_HEADS * QH_W)
    wuq_r = wuq_r.astype(BF16)
    wukv_h = w_ukv[layer].reshape(KV_LORA, N_HEADS, NOPE + VHEAD)
    wk_abs = jnp.transpose(wukv_h[..., 0:NOPE], (1, 2, 0)).astype(BF16)
    wv_up = jnp.transpose(wukv_h[..., NOPE:], (1, 0, 2)).astype(BF16)
    cos_t, sin_t = _rope_tables(s)
    u_b = peer_u[layer].astype(BF16)
    vt_b = peer_v[layer].T.astype(BF16)
    row = lambda v: v[layer][None, :]

    n_rows = -(-(b + 1) // 8) * 8
    cc = jnp.concatenate([c, c_ctx[None, :], jnp.zeros((n_rows - b - 1, D_MODEL), F32)], axis=0)
    mod_all = _mod_call(cc, w_ada[layer], b_ada[layer][None, :])
    mod = mod_all[:b].reshape(b, N_MOD, D_MODEL)
    mod_c = mod_all[b].reshape(N_MOD, D_MODEL)

    k_ctx = _ctx_call(ctx, mod_c, row(g_mix_pre), win_r[:, 0:WIN_CQ], row(g_kv_lora))
    k_lat, q, pool_in, gates = _proj_call(x, mod, row(g_mix_pre), win_r, row(g_q_lora), wuq_r, row(g_kv_lora),
                                          wk_abs, cos_t, sin_t)
    k_all = jnp.concatenate([k_ctx, k_lat], axis=1)
    attn = _attn_call(q, k_all, wv_up)
    ypool = _pool_call(pool_in, w_pool[layer].astype(BF16), row(pool_scale))
    x_mid, hf, s2t, e2t, cutt, e1t = _merge_call(
        x, attn, ypool, gates, mod, w_out[layer].astype(BF16), row(g_mix_post), row(g_ffn_pre),
        peer_wq[layer].astype(BF16), peer_keys[layer].astype(BF16))
    return _peer_call(hf, u_b, vt_b, s2t, e2t, cutt, e1t, x_mid, mod, row(g_ffn_post))
```
